```python
import jax, jax.numpy as jnp
from jax import lax
import numpy as np

D_MODEL = 1024
BATCH = 1
SEQ = 16384
DEPTH = 1

GLA_HEADS = 4
GLA_DK = 128
GLA_DV = 256
GLA_KEY = GLA_HEADS * GLA_DK
GLA_VAL = GLA_HEADS * GLA_DV
GLA_GATE_RANK = 16
GLA_GATE_NORM = 16.0
GLA_CHUNK = 64
LRU_WIDTH = 1344
LRU_BLOCKS = 8
LRU_BW = LRU_WIDTH // LRU_BLOCKS
LRU_CONV = 4
LRU_C = 8.0
LRU_MIN_RAD = 0.9
LRU_MAX_RAD = 0.999
D_FF = 3 * D_MODEL
FFN_CONV = 3
EPS = 1e-6

D_PROJ = 2 * GLA_KEY + 2 * GLA_VAL + GLA_GATE_RANK + 2 * LRU_WIDTH + 2 * D_MODEL

kernel_name = "hybrid_gla_rglru_convffn"


def rms_norm(x, g):
    xf = x.astype(jnp.float32)
    y = xf * lax.rsqrt(jnp.mean(xf * xf, axis=-1, keepdims=True) + EPS)
    return (y * g.astype(jnp.float32)).astype(x.dtype)


def causal_dwconv(x, w, b):
    K = w.shape[0]
    S = x.shape[1]
    xp = jnp.pad(x, ((0, 0), (K - 1, 0), (0, 0)))
    y = b
    for j in range(K):
        y = y + xp[:, j:j + S] * w[j]
    return y


def gla_chunked(q, k, v, gk):
    B, S, H, dk = q.shape
    dv = v.shape[-1]
    C = GLA_CHUNK
    N = S // C
    out_dtype = v.dtype

    def to_chunks(t):
        return t.astype(jnp.float32).reshape(B, N, C, H, -1).transpose(0, 3, 1, 2, 4)

    qc = to_chunks(q) * (dk ** -0.5)
    kc = to_chunks(k)
    vc = to_chunks(v)
    b = jnp.cumsum(to_chunks(gk), axis=3)
    b_last = b[:, :, :, -1:]
    q_t = qc * jnp.exp(b)
    k_t = kc * jnp.exp(-b)
    k_s = kc * jnp.exp(b_last - b)

    mask = jnp.tril(jnp.ones((C, C), dtype=bool))
    A = jnp.einsum('bhnik,bhnjk->bhnij', q_t, k_t)
    A = jnp.where(mask, A, 0.0)
    o_intra = jnp.einsum('bhnij,bhnjv->bhniv', A, vc)

    dS = jnp.einsum('bhnck,bhncv->bhnkv', k_s, vc)
    decay = jnp.exp(b_last[:, :, :, 0])

    def step(s_prev, inp):
        d, ds = inp
        return d[..., None] * s_prev + ds, s_prev

    s0 = jnp.zeros((B, H, dk, dv), jnp.float32)
    _, s_start = lax.scan(step, s0, (jnp.moveaxis(decay, 2, 0), jnp.moveaxis(dS, 2, 0)))
    s_start = jnp.moveaxis(s_start, 0, 2)
    o_inter = jnp.einsum('bhnck,bhnkv->bhncv', q_t, s_start)

    o = o_intra + o_inter
    return o.transpose(0, 2, 3, 1, 4).reshape(B, S, H, dv).astype(out_dtype)


def rg_lru(x, w_a, b_a, w_x, b_x, a_param):
    B, S, W = x.shape
    xf = x.astype(jnp.float32)
    xb = xf.reshape(B, S, LRU_BLOCKS, LRU_BW)
    r = jax.nn.sigmoid(jnp.einsum('bsnc,ncd->bsnd', xb, w_a.astype(jnp.float32)).reshape(B, S, W) + b_a)
    i = jax.nn.sigmoid(jnp.einsum('bsnc,ncd->bsnd', xb, w_x.astype(jnp.float32)).reshape(B, S, W) + b_x)
    log_a = -LRU_C * r * jax.nn.softplus(a_param.astype(jnp.float32))
    a = jnp.exp(log_a)
    u = jnp.sqrt(-jnp.expm1(2.0 * log_a)) * (i * xf)

    def combine(c1, c2):
        a1, b1 = c1
        a2, b2 = c2
        return a1 * a2, a2 * b1 + b2

    _, h = lax.associative_scan(combine, (a, u), axis=1)
    return h.astype(x.dtype)


def setup_inputs(seed: int = 0) -> dict:
    key = jax.random.key(seed)
    ks = jax.random.split(key, 24)
    L = DEPTH

    def nrm(k, shape, fan_in):
        return jax.random.normal(k, shape, jnp.float32) * (fan_in ** -0.5)

    def gain(k, shape):
        return 1.0 + 0.02 * jax.random.normal(k, shape, jnp.float32)

    def small(k, shape):
        return 0.01 * jax.random.normal(k, shape, jnp.float32)

    x = jax.random.normal(ks[0], (BATCH, SEQ, D_MODEL), jnp.float32)
    rad = jax.random.uniform(ks[13], (L, LRU_WIDTH), jnp.float32, LRU_MIN_RAD, LRU_MAX_RAD)
    lru_a_param = jnp.log(jnp.expm1(-jnp.log(rad)))
    return {
        "x": x,
        "attn_norm": gain(ks[1], (L, D_MODEL)),
        "w_in": nrm(ks[2], (L, D_MODEL, D_PROJ), D_MODEL),
        "gla_wf2": nrm(ks[3], (L, GLA_GATE_RANK, GLA_KEY), GLA_GATE_RANK),
        "gla_bf2": small(ks[4], (L, GLA_KEY)),
        "gla_onorm": gain(ks[5], (L, GLA_DV)),
        "gla_wo": nrm(ks[6], (L, GLA_VAL, D_MODEL), GLA_VAL),
        "lru_conv_w": nrm(ks[7], (L, LRU_CONV, LRU_WIDTH), LRU_CONV),
        "lru_conv_b": small(ks[8], (L, LRU_WIDTH)),
        "lru_wa": nrm(ks[9], (L, LRU_BLOCKS, LRU_BW, LRU_BW), LRU_BW),
        "lru_ba": small(ks[10], (L, LRU_WIDTH)),
        "lru_wx": nrm(ks[11], (L, LRU_BLOCKS, LRU_BW, LRU_BW), LRU_BW),
        "lru_bx": small(ks[12], (L, LRU_WIDTH)),
        "lru_a_param": lru_a_param,
        "lru_wo": nrm(ks[14], (L, LRU_WIDTH, D_MODEL), LRU_WIDTH),
        "w_out": nrm(ks[15], (L, D_MODEL, D_MODEL), D_MODEL),
        "ffn_norm": gain(ks[16], (L, D_MODEL)),
        "ffn_wup": nrm(ks[17], (L, D_MODEL, 2 * D_FF), D_MODEL),
        "ffn_conv_w": nrm(ks[18], (L, FFN_CONV, 2 * D_FF), FFN_CONV),
        "ffn_conv_b": small(ks[19], (L, 2 * D_FF)),
        "ffn_wdown": nrm(ks[20], (L, D_FF, D_MODEL), D_FF),
        "final_norm": gain(ks[21], (D_MODEL,)),
    }


def reference(x, attn_norm, w_in, gla_wf2, gla_bf2, gla_onorm, gla_wo,
              lru_conv_w, lru_conv_b, lru_wa, lru_ba, lru_wx, lru_bx, lru_a_param, lru_wo,
              w_out, ffn_norm, ffn_wup, ffn_conv_w, ffn_conv_b, ffn_wdown, final_norm):
    B, S, _ = x.shape
    splits = np.cumsum([GLA_KEY, GLA_KEY, GLA_VAL, GLA_VAL, GLA_GATE_RANK,
                        LRU_WIDTH, LRU_WIDTH, D_MODEL])
    for l in range(DEPTH):
        h = rms_norm(x, attn_norm[l])
        p = jnp.einsum('bsd,de->bse', h, w_in[l])
        q, k, v, g, f_low, x_lru, lru_gate, gate_a, gate_b = jnp.split(p, splits, axis=-1)

        gk = jax.nn.log_sigmoid(
            (jnp.einsum('bsr,rk->bsk', f_low, gla_wf2[l]) + gla_bf2[l]).astype(jnp.float32)) / GLA_GATE_NORM
        o = gla_chunked(q.reshape(B, S, GLA_HEADS, GLA_DK),
                        k.reshape(B, S, GLA_HEADS, GLA_DK),
                        v.reshape(B, S, GLA_HEADS, GLA_DV),
                        gk.reshape(B, S, GLA_HEADS, GLA_DK))
        o = rms_norm(o, gla_onorm[l]) * jax.nn.silu(g.reshape(B, S, GLA_HEADS, GLA_DV))
        y_a = jnp.einsum('bse,ed->bsd', o.reshape(B, S, GLA_VAL), gla_wo[l])

        xc = causal_dwconv(x_lru, lru_conv_w[l], lru_conv_b[l])
        hl = rg_lru(xc, lru_wa[l], lru_ba[l], lru_wx[l], lru_bx[l], lru_a_param[l])
        y_b = jnp.einsum('bse,ed->bsd', hl * jax.nn.gelu(lru_gate, approximate=True), lru_wo[l])

        merged = jax.nn.sigmoid(gate_a) * y_a + jax.nn.sigmoid(gate_b) * y_b
        x = x + jnp.einsum('bsd,de->bse', merged, w_out[l])

        h = rms_norm(x, ffn_norm[l])
        u = jnp.einsum('bsd,df->bsf', h, ffn_wup[l])
        u = causal_dwconv(u, ffn_conv_w[l], ffn_conv_b[l])
        u_gate, u_val = jnp.split(u, 2, axis=-1)
        x = x + jnp.einsum('bsf,fd->bsd', jax.nn.gelu(u_gate, approximate=True) * u_val, ffn_wdown[l])

    return rms_norm(x, final_norm)
```

```python
import functools

import jax
import jax.numpy as jnp
import numpy as np
from jax import lax
from jax.experimental import pallas as pl
from jax.experimental.pallas import tpu as pltpu

D_MODEL = 1024
GLA_HEADS = 4
GLA_DK = 128
GLA_DV = 256
GLA_KEY = GLA_HEADS * GLA_DK
GLA_VAL = GLA_HEADS * GLA_DV
GLA_GATE_RANK = 16
GLA_GATE_NORM = 16.0
GLA_CHUNK = 64
LRU_WIDTH = 1344
LRU_BLOCKS = 8
LRU_BW = LRU_WIDTH // LRU_BLOCKS
LRU_CONV = 4
LRU_C = 8.0
D_FF = 3 * D_MODEL
FFN_CONV = 3
EPS = 1e-6

LANES = 128
SUBLANES = 8
MXU_DIM = 256
VMEM_LIMIT_BYTES = 58 * 1024 * 1024

LRU_PAD = ((LRU_WIDTH + LANES - 1) // LANES) * LANES
RANK_PAD = LANES
HALO = SUBLANES
SEQ_BLOCK = 256

BF16 = jnp.bfloat16
F32 = jnp.float32


def _band_tiles():
    tiles = []
    for c0 in range(0, LRU_PAD, MXU_DIM):
        c1 = min(c0 + MXU_DIM, LRU_PAD)
        last_col = min(c1, LRU_WIDTH) - 1
        b_lo = c0 // LRU_BW
        b_hi = last_col // LRU_BW
        k0 = (b_lo * LRU_BW) // LANES * LANES
        k1 = min(-(-((b_hi + 1) * LRU_BW) // LANES) * LANES, LRU_PAD)
        tiles.append((k0, k1, c0, c1))
    return tuple(tiles)


BAND_TILES = _band_tiles()
BAND_ROWS = sum(k1 - k0 for k0, k1, _, _ in BAND_TILES)


def _dot(a, b):
    return jnp.dot(a, b, preferred_element_type=F32)


def _dot_nt(a, b):
    return lax.dot_general(a, b, (((1,), (1,)), ((), ())), preferred_element_type=F32)


def _dot_tn(a, b):
    return lax.dot_general(a, b, (((0,), (0,)), ((), ())), preferred_element_type=F32)


def _sigmoid(x):
    return 1.0 / (1.0 + jnp.exp(-x))


def _softplus(x):
    return jnp.maximum(x, 0.0) + jnp.log1p(jnp.exp(-jnp.abs(x)))


def _gelu_tanh(x):
    c = np.sqrt(2.0 / np.pi).astype(np.float32)
    return 0.5 * x * (1.0 + jnp.tanh(c * (x + 0.044715 * (x * x * x))))


def _rms_norm(x, g):
    ms = jnp.mean(x * x, axis=-1, keepdims=True)
    return x * lax.rsqrt(ms + EPS) * g


def _mixer_kernel(x_ref, an_ref, wq_ref, wk_ref, wv_ref, wg_ref, wf_ref, wxl_ref, wlg_ref, wga_ref, wgb_ref,
                  wf2_ref, bf2_ref, onorm_ref, gwo_ref, cw_ref, cb_ref, wab_ref, ba_ref, wxb_ref, bx_ref,
                  ap_ref, lwo_ref, wout_ref,
                  out_ref,
                  st_ref, xl_ref, hc_ref, o_ref, h_ref):
    T = x_ref.shape[0]
    n_chunks = T // GLA_CHUNK

    @pl.when(pl.program_id(0) == 0)
    def _():
        st_ref[...] = jnp.zeros_like(st_ref)
        xl_ref[0:HALO, :] = jnp.zeros((HALO, LRU_PAD), F32)
        hc_ref[...] = jnp.zeros_like(hc_ref)

    x = x_ref[...]
    hb = _rms_norm(x, an_ref[...]).astype(BF16)

    q = _dot(hb, wq_ref[...]) * (GLA_DK ** -0.5)
    k = _dot(hb, wk_ref[...])
    v = _dot(hb, wv_ref[...]).astype(BF16)
    f_low = _dot(hb, wf_ref[...]).astype(BF16)
    z = _dot(f_low, wf2_ref[...]) + bf2_ref[...]
    gk = (jnp.minimum(z, 0.0) - jnp.log1p(jnp.exp(-jnp.abs(z)))) * (1.0 / GLA_GATE_NORM)

    row = lax.broadcasted_iota(jnp.int32, (T, T), 0)
    col = lax.broadcasted_iota(jnp.int32, (T, T), 1)
    tri = jnp.where((row >= col) & ((row // GLA_CHUNK) == (col // GLA_CHUNK)), 1.0, 0.0).astype(BF16)
    g1 = gk.astype(BF16)
    r1 = gk - g1.astype(F32)
    g2 = r1.astype(BF16)
    g3 = (r1 - g2.astype(F32)).astype(BF16)
    b = _dot(tri, g1) + _dot(tri, g2) + _dot(tri, g3)

    b_last_rows = [b[(c + 1) * GLA_CHUNK - 1:(c + 1) * GLA_CHUNK, :] for c in range(n_chunks)]
    b_last = jnp.concatenate([jnp.broadcast_to(r, (GLA_CHUNK, GLA_KEY)) for r in b_last_rows], axis=0)
    q_t = (q * jnp.exp(b)).astype(BF16)
    k_t = (k * jnp.exp(-b)).astype(BF16)
    k_s = (k * jnp.exp(b_last - b)).astype(BF16)

    crow = lax.broadcasted_iota(jnp.int32, (GLA_CHUNK, GLA_CHUNK), 0)
    ccol = lax.broadcasted_iota(jnp.int32, (GLA_CHUNK, GLA_CHUNK), 1)
    causal = crow >= ccol
    for c in range(n_chunks):
        rs = slice(c * GLA_CHUNK, (c + 1) * GLA_CHUNK)
        decay = jnp.exp(b_last_rows[c])
        for hh in range(GLA_HEADS):
            ks_ = slice(hh * GLA_DK, (hh + 1) * GLA_DK)
            vs_ = slice(hh * GLA_DV, (hh + 1) * GLA_DV)
            qt = q_t[rs, ks_]
            vc = v[rs, vs_]
            a = jnp.where(causal, _dot_nt(qt, k_t[rs, ks_]), 0.0).astype(BF16)
            st = st_ref[hh]
            o_ref[rs, vs_] = _dot(a, vc) + _dot_nt(qt, st.astype(BF16))
            st_ref[hh] = st * decay[:, ks_] + _dot_tn(vc, k_s[rs, ks_])

    g = _dot(hb, wg_ref[...])
    og = []
    for hh in range(GLA_HEADS):
        vs_ = slice(hh * GLA_DV, (hh + 1) * GLA_DV)
        gh = g[:, vs_]
        og.append(_rms_norm(o_ref[:, vs_], onorm_ref[...]) * (gh * _sigmoid(gh)))
    y_a = _dot(jnp.concatenate(og, axis=1).astype(BF16), gwo_ref[...])

    xl_ref[HALO:HALO + T, :] = _dot(hb, wxl_ref[...])
    xc = cb_ref[...]
    for j in range(LRU_CONV):
        xc = xc + xl_ref[pl.ds(HALO - (LRU_CONV - 1) + j, T), :] * cw_ref[j:j + 1, :]
    xl_ref[0:HALO, :] = xl_ref[T:T + HALO, :]

    xcb = xc.astype(BF16)
    row0 = 0
    r_parts, i_parts = [], []
    for (k0, k1, c0, c1) in BAND_TILES:
        xw = xcb[:, k0:k1]
        r_parts.append(_dot(xw, wab_ref[row0:row0 + (k1 - k0), 0:c1 - c0]))
        i_parts.append(_dot(xw, wxb_ref[row0:row0 + (k1 - k0), 0:c1 - c0]))
        row0 += k1 - k0
    r_gate = _sigmoid(jnp.concatenate(r_parts, axis=1) + ba_ref[...])
    i_gate = _sigmoid(jnp.concatenate(i_parts, axis=1) + bx_ref[...])
    log_a = (-LRU_C) * r_gate * _softplus(ap_ref[...])
    a_s = jnp.exp(log_a)
    u_s = jnp.sqrt(-jnp.tanh(log_a) * (1.0 + a_s * a_s)) * (i_gate * xc)

    rmod = lax.broadcasted_iota(jnp.int32, (T, LRU_PAD), 0) % SUBLANES
    s = 1
    while s < SUBLANES:
        m = rmod >= s
        u_s = jnp.where(m, u_s + a_s * pltpu.roll(u_s, s, 0), u_s)
        a_s = jnp.where(m, a_s * pltpu.roll(a_s, s, 0), a_s)
        s *= 2
    h_prev = hc_ref[...]
    for gi in range(T // SUBLANES):
        rs = slice(gi * SUBLANES, (gi + 1) * SUBLANES)
        hg = u_s[rs, :] + a_s[rs, :] * h_prev
        h_ref[rs, :] = hg
        h_prev = hg[SUBLANES - 1:SUBLANES, :]
    hc_ref[...] = h_prev

    lg = _dot(hb, wlg_ref[...])
    y_b = _dot((h_ref[...] * _gelu_tanh(lg)).astype(BF16), lwo_ref[...])

    gate_a = _sigmoid(_dot(hb, wga_ref[...]))
    gate_b = _sigmoid(_dot(hb, wgb_ref[...]))
    merged = (gate_a * y_a + gate_b * y_b).astype(BF16)
    out_ref[...] = x + _dot(merged, wout_ref[...])


def _ffn_kernel(x_ref, fn_ref, wup_ref, cw_ref, cb_ref, wdown_ref, final_ref, out_ref, u_ref):
    T = x_ref.shape[0]

    @pl.when(pl.program_id(0) == 0)
    def _():
        u_ref[0:HALO, :] = jnp.zeros((HALO, 2 * D_FF), F32)

    x = x_ref[...]
    hb = _rms_norm(x, fn_ref[...]).astype(BF16)
    u_ref[HALO:HALO + T, :] = _dot(hb, wup_ref[...])
    uc = cb_ref[...]
    for j in range(FFN_CONV):
        uc = uc + u_ref[pl.ds(HALO - (FFN_CONV - 1) + j, T), :] * cw_ref[j:j + 1, :]
    u_ref[0:HALO, :] = u_ref[T:T + HALO, :]
    act = (_gelu_tanh(uc[:, :D_FF]) * uc[:, D_FF:]).astype(BF16)
    y = x + _dot(act, wdown_ref[...])
    out_ref[...] = _rms_norm(y, final_ref[...])


def _const_spec(shape):
    nd = len(shape)
    return pl.BlockSpec(shape, lambda i, _nd=nd: (0,) * _nd, pipeline_mode=pl.Buffered(1))


def _pad_cols(w, n):
    return jnp.pad(w, ((0, 0), (0, n - w.shape[1])))


def _band_pack(w):
    dense = jnp.zeros((LRU_PAD, LRU_PAD), F32)
    for n in range(LRU_BLOCKS):
        dense = lax.dynamic_update_slice(dense, w[n], (n * LRU_BW, n * LRU_BW))
    tiles = [_pad_cols(dense[k0:k1, c0:c1], MXU_DIM) for (k0, k1, c0, c1) in BAND_TILES]
    return jnp.concatenate(tiles, axis=0).astype(BF16)


def _row(v, n=None):
    v = v.reshape(1, -1).astype(F32)
    return v if n is None else _pad_cols(v, n)


@jax.jit
def _forward(x, attn_norm, w_in, gla_wf2, gla_bf2, gla_onorm, gla_wo, lru_conv_w, lru_conv_b, lru_wa, lru_ba,
             lru_wx, lru_bx, lru_a_param, lru_wo, w_out, ffn_norm, ffn_wup, ffn_conv_w, ffn_conv_b, ffn_wdown,
             final_norm):
    B, S, D = x.shape
    assert B == 1 and D == D_MODEL and S % SEQ_BLOCK == 0 and attn_norm.shape[0] == 1
    T = SEQ_BLOCK
    x2 = x.reshape(S, D)

    splits = np.cumsum([0, GLA_KEY, GLA_KEY, GLA_VAL, GLA_VAL, GLA_GATE_RANK, LRU_WIDTH, LRU_WIDTH, D_MODEL, D_MODEL])
    wi = w_in[0]
    pieces = [wi[:, splits[n]:splits[n + 1]] for n in range(9)]
    wq, wk, wv, wg, wf, wxl, wlg, wga, wgb = pieces
    wf = _pad_cols(wf, RANK_PAD)
    wxl = _pad_cols(wxl, LRU_PAD)
    wlg = _pad_cols(wlg, LRU_PAD)
    wf2 = jnp.pad(gla_wf2[0], ((0, RANK_PAD - GLA_GATE_RANK), (0, 0)))
    lwo = jnp.pad(lru_wo[0], ((0, LRU_PAD - LRU_WIDTH), (0, 0)))

    mixer_in = [
        x2, _row(attn_norm[0]),
        wq.astype(BF16), wk.astype(BF16), wv.astype(BF16), wg.astype(BF16), wf.astype(BF16),
        wxl.astype(BF16), wlg.astype(BF16), wga.astype(BF16), wgb.astype(BF16),
        wf2.astype(BF16), _row(gla_bf2[0]), _row(gla_onorm[0]), gla_wo[0].astype(BF16),
        _pad_cols(lru_conv_w[0].astype(F32), LRU_PAD), _row(lru_conv_b[0], LRU_PAD),
        _band_pack(lru_wa[0]), _row(lru_ba[0], LRU_PAD), _band_pack(lru_wx[0]), _row(lru_bx[0], LRU_PAD),
        _row(lru_a_param[0], LRU_PAD), lwo.astype(BF16), w_out[0].astype(BF16),
    ]
    blk = pl.BlockSpec((T, D), lambda i: (i, 0))
    params = pltpu.CompilerParams(dimension_semantics=("arbitrary",), vmem_limit_bytes=VMEM_LIMIT_BYTES)
    x1 = pl.pallas_call(
        _mixer_kernel,
        grid=(S // T,),
        in_specs=[blk] + [_const_spec(a.shape) for a in mixer_in[1:]],
        out_specs=blk,
        out_shape=jax.ShapeDtypeStruct((S, D), F32),
        scratch_shapes=[
            pltpu.VMEM((GLA_HEADS, GLA_DV, GLA_DK), F32),
            pltpu.VMEM((HALO + T, LRU_PAD), F32),
            pltpu.VMEM((1, LRU_PAD), F32),
            pltpu.VMEM((T, GLA_VAL), F32),
            pltpu.VMEM((T, LRU_PAD), F32),
        ],
        compiler_params=params,
        name="mixer",
    )(*mixer_in)

    ffn_in = [
        x1, _row(ffn_norm[0]), ffn_wup[0].astype(BF16), ffn_conv_w[0].astype(F32), _row(ffn_conv_b[0]),
        ffn_wdown[0].astype(BF16), _row(final_norm),
    ]
    out = pl.pallas_call(
        _ffn_kernel,
        grid=(S // T,),
        in_specs=[blk] + [_const_spec(a.shape) for a in ffn_in[1:]],
        out_specs=blk,
        out_shape=jax.ShapeDtypeStruct((S, D), F32),
        scratch_shapes=[pltpu.VMEM((HALO + T, 2 * D_FF), F32)],
        compiler_params=params,
        name="ffn",
    )(*ffn_in)
    return out.reshape(B, S, D)


def kernel(x, attn_norm, w_in, gla_wf2, gla_bf2, gla_onorm, gla_wo, lru_conv_w, lru_conv_b, lru_wa, lru_ba, lru_wx, lru_bx, lru_a_param, lru_wo, w_out, ffn_norm, ffn_wup, ffn_conv_w, ffn_conv_b, ffn_wdown, final_norm):
    return _forward(x, attn_norm, w_in, gla_wf2, gla_bf2, gla_onorm, gla_wo, lru_conv_w, lru_conv_b, lru_wa, lru_ba,
                    lru_wx, lru_bx, lru_a_param, lru_wo, w_out, ffn_norm, ffn_wup, ffn_conv_w, ffn_conv_b,
                    ffn_wdown, final_norm)
```

```python
import jax
import jax.numpy as jnp
import numpy as np
from jax import lax
from jax.experimental import pallas as pl
from jax.experimental.pallas import tpu as pltpu

D_MODEL = 1024
GLA_HEADS = 4
GLA_DK = 128
GLA_DV = 256
GLA_KEY = GLA_HEADS * GLA_DK
GLA_VAL = GLA_HEADS * GLA_DV
GLA_GATE_RANK = 16
GLA_GATE_NORM = 16.0
GLA_CHUNK = 64
LRU_WIDTH = 1344
LRU_BLOCKS = 8
LRU_BW = LRU_WIDTH // LRU_BLOCKS
LRU_CONV = 4
LRU_C = 8.0
D_FF = 3 * D_MODEL
FFN_CONV = 3
EPS = 1e-6

LANES = 128
SUBLANES = 8
MXU_DIM = 256
VMEM_LIMIT_BYTES = 58 * 1024 * 1024

LRU_PAD = ((LRU_WIDTH + LANES - 1) // LANES) * LANES
RANK_PAD = LANES
SEQ_BLOCK = 256
LRU_HALO = (LRU_CONV - 1) * SUBLANES
FFN_HALO = (FFN_CONV - 1) * SUBLANES

_IN_WIDTHS = (GLA_KEY, GLA_KEY, GLA_VAL, GLA_VAL, RANK_PAD, LRU_PAD, LRU_PAD, D_MODEL, D_MODEL)
_IN_OFFS = tuple(int(v) for v in np.cumsum((0,) + _IN_WIDTHS))
(C_Q, C_K, C_V, C_G, C_F, C_XL, C_LG, C_GA, C_GB) = [slice(_IN_OFFS[n], _IN_OFFS[n + 1]) for n in range(9)]
D_PROJ_PAD = _IN_OFFS[-1]

BF16 = jnp.bfloat16
F32 = jnp.float32


def _band_tiles():
    tiles = []
    for c0 in range(0, LRU_PAD, MXU_DIM):
        c1 = min(c0 + MXU_DIM, LRU_PAD)
        last_col = min(c1, LRU_WIDTH) - 1
        b_lo = c0 // LRU_BW
        b_hi = last_col // LRU_BW
        k0 = (b_lo * LRU_BW) // LANES * LANES
        k1 = min(-(-((b_hi + 1) * LRU_BW) // LANES) * LANES, LRU_PAD)
        tiles.append((k0, k1, c0, c1))
    return tuple(tiles)


BAND_TILES = _band_tiles()
BAND_ROWS = sum(k1 - k0 for k0, k1, _, _ in BAND_TILES)


def _dot(a, b):
    return jnp.dot(a, b, preferred_element_type=F32)


def _dot_nt(a, b):
    return lax.dot_general(a, b, (((1,), (1,)), ((), ())), preferred_element_type=F32)


def _dot_tn(a, b):
    return lax.dot_general(a, b, (((0,), (0,)), ((), ())), preferred_element_type=F32)


def _sigmoid(x):
    return 1.0 / (1.0 + jnp.exp(-x))


def _softplus(x):
    return jnp.maximum(x, 0.0) + jnp.log1p(jnp.exp(-jnp.abs(x)))


def _gelu_tanh(x):
    c = float(np.sqrt(2.0 / np.pi))
    hx = 0.5 * x
    return hx + hx * jnp.tanh(x * (c + (0.044715 * c) * (x * x)))


def _rms_norm(x, g):
    ms = jnp.mean(x * x, axis=-1, keepdims=True)
    return x * lax.rsqrt(ms + EPS) * g


def _segment_major(hb):
    T = hb.shape[0]
    seg_len = T // SUBLANES
    p = lax.broadcasted_iota(jnp.int32, (T, T), 0)
    t = lax.broadcasted_iota(jnp.int32, (T, T), 1)
    perm = jnp.where(t == (p % SUBLANES) * seg_len + p // SUBLANES, 1.0, 0.0).astype(BF16)
    return _dot(perm, hb).astype(BF16)


def _fill_halo(buf_ref, prev_ref, cur, halo):
    T = cur.shape[0]
    sub = lax.broadcasted_iota(jnp.int32, (SUBLANES, cur.shape[1]), 0)
    for off in range(0, halo, SUBLANES):
        rows = slice(off, off + SUBLANES)
        tail = cur[T - halo + off:T - halo + off + SUBLANES, :]
        buf_ref[rows, :] = pltpu.roll(jnp.where(sub == SUBLANES - 1, prev_ref[rows, :], tail), 1, 0)
    prev_ref[...] = cur[T - halo:T, :]


def _to_token_major(val, slab_ref, dst_ref):
    T, d = val.shape
    seg_len = T // SUBLANES
    for n in range(d // LANES):
        slab_ref[n] = val[:, n * LANES:(n + 1) * LANES]
    for j in range(T // SUBLANES):
        t0 = j * SUBLANES
        start = (t0 % seg_len) * SUBLANES + t0 // seg_len
        for n in range(d // LANES):
            dst_ref[t0:t0 + SUBLANES, n * LANES:(n + 1) * LANES] = slab_ref[n, pl.ds(start, SUBLANES, stride=SUBLANES), :]


def _mixer_kernel(x_ref, an_ref, win_ref, wf2_ref, bf2_ref, onorm_ref, gwo_ref, cw_ref, cb_ref, wab_ref, ba_ref,
                  wxb_ref, bx_ref, ap_ref, lwo_ref, wout_ref,
                  out_ref,
                  st_ref, xl_ref, xprev_ref, hc_ref, o_ref, hl_ref, ac_ref, slab_ref, mb_ref):
    T = x_ref.shape[0]
    n_chunks = T // GLA_CHUNK
    seg_len = T // SUBLANES

    @pl.when(pl.program_id(0) == 0)
    def _():
        st_ref[...] = jnp.zeros_like(st_ref)
        xprev_ref[...] = jnp.zeros_like(xprev_ref)
        hc_ref[...] = jnp.zeros_like(hc_ref)

    x = x_ref[...]
    hb = _rms_norm(x, an_ref[...]).astype(BF16)

    q = _dot(hb, win_ref[:, C_Q]) * (GLA_DK ** -0.5)
    k = _dot(hb, win_ref[:, C_K])
    v = _dot(hb, win_ref[:, C_V]).astype(BF16)
    f_low = _dot(hb, win_ref[:, C_F]).astype(BF16)
    z = _dot(f_low, wf2_ref[...]) + bf2_ref[...]
    gk = (jnp.minimum(z, 0.0) - jnp.log1p(jnp.exp(-jnp.abs(z)))) * (1.0 / GLA_GATE_NORM)

    row = lax.broadcasted_iota(jnp.int32, (T, T), 0)
    col = lax.broadcasted_iota(jnp.int32, (T, T), 1)
    intra = (row >= col) & ((row // GLA_CHUNK) == (col // GLA_CHUNK))
    tri = jnp.where(intra, 1.0, 0.0).astype(BF16)
    g1 = gk.astype(BF16)
    r1 = gk - g1.astype(F32)
    g2 = r1.astype(BF16)
    g3 = (r1 - g2.astype(F32)).astype(BF16)
    b = _dot(tri, g1) + _dot(tri, g2) + _dot(tri, g3)

    b_last_rows = [b[(c + 1) * GLA_CHUNK - 1:(c + 1) * GLA_CHUNK, :] for c in range(n_chunks)]
    b_last = jnp.concatenate([jnp.broadcast_to(r, (GLA_CHUNK, GLA_KEY)) for r in b_last_rows], axis=0)
    decay = [jnp.exp(r) for r in b_last_rows]
    q_t = (q * jnp.exp(b)).astype(BF16)
    k_t = (k * jnp.exp(-b)).astype(BF16)
    k_s = (k * jnp.exp(b_last - b)).astype(BF16)

    def chunk_expand(m):
        cols = []
        for c in range(n_chunks):
            parts = []
            if c > 0:
                parts.append(jnp.zeros((c * GLA_CHUNK, GLA_DK), m.dtype))
            parts.append(m[c * GLA_CHUNK:(c + 1) * GLA_CHUNK, :])
            if c < n_chunks - 1:
                parts.append(jnp.zeros((T - (c + 1) * GLA_CHUNK, GLA_DK), m.dtype))
            cols.append(jnp.concatenate(parts, axis=0))
        return jnp.concatenate(cols, axis=1)

    for hh in range(GLA_HEADS):
        ks_ = slice(hh * GLA_DK, (hh + 1) * GLA_DK)
        vs_ = slice(hh * GLA_DV, (hh + 1) * GLA_DV)
        qh = q_t[:, ks_]
        vh = v[:, vs_]
        a = jnp.where(intra, _dot_nt(qh, k_t[:, ks_]), 0.0).astype(BF16)
        ds_all = _dot_tn(vh, chunk_expand(k_s[:, ks_]))
        st = st_ref[hh]
        starts = []
        for c in range(n_chunks):
            starts.append(st.astype(BF16))
            st = st * decay[c][:, ks_] + ds_all[:, c * GLA_DK:(c + 1) * GLA_DK]
        st_ref[hh] = st
        o_ref[:, vs_] = _dot(a, vh) + _dot_nt(chunk_expand(qh), jnp.concatenate(starts, axis=1))

    g = _dot(hb, win_ref[:, C_G])
    og = []
    for hh in range(GLA_HEADS):
        vs_ = slice(hh * GLA_DV, (hh + 1) * GLA_DV)
        gh = g[:, vs_]
        og.append(_rms_norm(o_ref[:, vs_], onorm_ref[...]) * (gh * _sigmoid(gh)))
    y_a = _dot(jnp.concatenate(og, axis=1).astype(BF16), gwo_ref[...])

    hbp = _segment_major(hb)
    xl = _dot(hbp, win_ref[:, C_XL])
    xl_ref[LRU_HALO:LRU_HALO + T, :] = xl
    _fill_halo(xl_ref, xprev_ref, xl, LRU_HALO)
    xc = cb_ref[...] + xl * cw_ref[LRU_CONV - 1:LRU_CONV, :]
    for d in range(1, LRU_CONV):
        xc = xc + xl_ref[LRU_HALO - d * SUBLANES:LRU_HALO - d * SUBLANES + T, :] * cw_ref[LRU_CONV - 1 - d:LRU_CONV - d, :]

    xcb = xc.astype(BF16)
    row0 = 0
    r_parts, i_parts = [], []
    for (k0, k1, c0, c1) in BAND_TILES:
        xw = xcb[:, k0:k1]
        r_parts.append(_dot(xw, wab_ref[row0:row0 + (k1 - k0), 0:c1 - c0]))
        i_parts.append(_dot(xw, wxb_ref[row0:row0 + (k1 - k0), 0:c1 - c0]))
        row0 += k1 - k0
    r_gate = _sigmoid(jnp.concatenate(r_parts, axis=1) + ba_ref[...])
    i_gate = _sigmoid(jnp.concatenate(i_parts, axis=1) + bx_ref[...])
    log_a = (-LRU_C) * r_gate * _softplus(ap_ref[...])
    a_s = jnp.exp(log_a)
    u_s = jnp.sqrt(-jnp.tanh(log_a) * (1.0 + a_s * a_s)) * (i_gate * xc)

    h_loc = u_s[0:SUBLANES, :]
    a_cum = a_s[0:SUBLANES, :]
    hl_ref[0:SUBLANES, :] = h_loc
    ac_ref[0:SUBLANES, :] = a_cum
    for r in range(1, seg_len):
        rs = slice(r * SUBLANES, (r + 1) * SUBLANES)
        h_loc = a_s[rs, :] * h_loc + u_s[rs, :]
        a_cum = a_s[rs, :] * a_cum
        hl_ref[rs, :] = h_loc
        ac_ref[rs, :] = a_cum
    sub = lax.broadcasted_iota(jnp.int32, (SUBLANES, LRU_PAD), 0)
    e_s, p_s = h_loc, a_cum
    s = 1
    while s < SUBLANES:
        m = sub >= s
        e_s = jnp.where(m, e_s + p_s * pltpu.roll(e_s, s, 0), e_s)
        p_s = jnp.where(m, p_s * pltpu.roll(p_s, s, 0), p_s)
        s *= 2
    h_carry = hc_ref[...]
    seg_end = e_s + p_s * h_carry
    seg_start = jnp.where(sub == 0, h_carry, pltpu.roll(seg_end, 1, 0))
    hc_ref[...] = seg_end[SUBLANES - 1:SUBLANES, :]
    for r in range(seg_len):
        rs = slice(r * SUBLANES, (r + 1) * SUBLANES)
        hl_ref[rs, :] = hl_ref[rs, :] + ac_ref[rs, :] * seg_start

    lg = _dot(hbp, win_ref[:, C_LG])
    y_b = _dot((hl_ref[...] * _gelu_tanh(lg)).astype(BF16), lwo_ref[...])
    gate_b = _sigmoid(_dot(hbp, win_ref[:, C_GB]))
    _to_token_major(gate_b * y_b, slab_ref, mb_ref)

    gate_a = _sigmoid(_dot(hb, win_ref[:, C_GA]))
    merged = (gate_a * y_a + mb_ref[...]).astype(BF16)
    out_ref[...] = x + _dot(merged, wout_ref[...])


def _ffn_kernel(x_ref, fn_ref, wup_ref, cw_ref, cb_ref, wdown_ref, final_ref, out_ref,
                u_ref, uprev_ref, slab_ref, y_ref):
    T = x_ref.shape[0]

    @pl.when(pl.program_id(0) == 0)
    def _():
        uprev_ref[...] = jnp.zeros_like(uprev_ref)

    x = x_ref[...]
    hbp = _segment_major(_rms_norm(x, fn_ref[...]).astype(BF16))
    u = _dot(hbp, wup_ref[...])
    u_ref[FFN_HALO:FFN_HALO + T, :] = u
    _fill_halo(u_ref, uprev_ref, u, FFN_HALO)
    uc = cb_ref[...] + u * cw_ref[FFN_CONV - 1:FFN_CONV, :]
    for d in range(1, FFN_CONV):
        uc = uc + u_ref[FFN_HALO - d * SUBLANES:FFN_HALO - d * SUBLANES + T, :] * cw_ref[FFN_CONV - 1 - d:FFN_CONV - d, :]
    act = (_gelu_tanh(uc[:, :D_FF]) * uc[:, D_FF:]).astype(BF16)
    _to_token_major(_dot(act, wdown_ref[...]), slab_ref, y_ref)
    out_ref[...] = _rms_norm(x + y_ref[...], final_ref[...])


def _const_spec(shape):
    nd = len(shape)
    return pl.BlockSpec(shape, lambda i, _nd=nd: (0,) * _nd, pipeline_mode=pl.Buffered(1))


def _pad_cols(w, n):
    return jnp.pad(w, ((0, 0), (0, n - w.shape[1])))


def _band_pack(w):
    eye = jnp.eye(LRU_BLOCKS, dtype=w.dtype)
    dense = (w[:, :, None, :] * eye[:, None, :, None]).reshape(LRU_WIDTH, LRU_WIDTH)
    dense = jnp.pad(dense, ((0, LRU_PAD - LRU_WIDTH), (0, LRU_PAD - LRU_WIDTH)))
    tiles = [_pad_cols(dense[k0:k1, c0:c1], MXU_DIM) for (k0, k1, c0, c1) in BAND_TILES]
    return jnp.concatenate(tiles, axis=0).astype(BF16)


def _row(v, n=None):
    v = v.reshape(1, -1).astype(F32)
    return v if n is None else _pad_cols(v, n)


@jax.jit
def _forward(x, attn_norm, w_in, gla_wf2, gla_bf2, gla_onorm, gla_wo, lru_conv_w, lru_conv_b, lru_wa, lru_ba,
             lru_wx, lru_bx, lru_a_param, lru_wo, w_out, ffn_norm, ffn_wup, ffn_conv_w, ffn_conv_b, ffn_wdown,
             final_norm):
    B, S, D = x.shape
    assert B == 1 and D == D_MODEL and S % SEQ_BLOCK == 0 and attn_norm.shape[0] == 1
    T = SEQ_BLOCK
    x2 = x.reshape(S, D)

    splits = np.cumsum([0, GLA_KEY, GLA_KEY, GLA_VAL, GLA_VAL, GLA_GATE_RANK, LRU_WIDTH, LRU_WIDTH, D_MODEL, D_MODEL])
    wi = w_in[0]
    w_in_packed = jnp.concatenate(
        [_pad_cols(wi[:, splits[n]:splits[n + 1]], _IN_WIDTHS[n]) for n in range(9)], axis=1).astype(BF16)
    wf2 = jnp.pad(gla_wf2[0], ((0, RANK_PAD - GLA_GATE_RANK), (0, 0)))
    lwo = jnp.pad(lru_wo[0], ((0, LRU_PAD - LRU_WIDTH), (0, 0)))

    mixer_in = [
        x2, _row(attn_norm[0]), w_in_packed,
        wf2.astype(BF16), _row(gla_bf2[0]), _row(gla_onorm[0]), gla_wo[0].astype(BF16),
        _pad_cols(lru_conv_w[0].astype(F32), LRU_PAD), _row(lru_conv_b[0], LRU_PAD),
        _band_pack(lru_wa[0]), _row(lru_ba[0], LRU_PAD), _band_pack(lru_wx[0]), _row(lru_bx[0], LRU_PAD),
        _row(lru_a_param[0], LRU_PAD), lwo.astype(BF16), w_out[0].astype(BF16),
    ]
    blk = pl.BlockSpec((T, D), lambda i: (i, 0))
    params = pltpu.CompilerParams(dimension_semantics=("arbitrary",), vmem_limit_bytes=VMEM_LIMIT_BYTES)
    x1 = pl.pallas_call(
        _mixer_kernel,
        grid=(S // T,),
        in_specs=[blk] + [_const_spec(a.shape) for a in mixer_in[1:]],
        out_specs=blk,
        out_shape=jax.ShapeDtypeStruct((S, D), F32),
        scratch_shapes=[
            pltpu.VMEM((GLA_HEADS, GLA_DV, GLA_DK), F32),
            pltpu.VMEM((LRU_HALO + T, LRU_PAD), F32),
            pltpu.VMEM((LRU_HALO, LRU_PAD), F32),
            pltpu.VMEM((1, LRU_PAD), F32),
            pltpu.VMEM((T, GLA_VAL), F32),
            pltpu.VMEM((T, LRU_PAD), F32),
            pltpu.VMEM((T, LRU_PAD), F32),
            pltpu.VMEM((D // LANES, T, LANES), F32),
            pltpu.VMEM((T, D), F32),
        ],
        compiler_params=params,
        name="mixer",
    )(*mixer_in)

    ffn_in = [
        x1, _row(ffn_norm[0]), ffn_wup[0].astype(BF16), ffn_conv_w[0].astype(F32), _row(ffn_conv_b[0]),
        ffn_wdown[0].astype(BF16), _row(final_norm),
    ]
    out = pl.pallas_call(
        _ffn_kernel,
        grid=(S // T,),
        in_specs=[blk] + [_const_spec(a.shape) for a in ffn_in[1:]],
        out_specs=blk,
        out_shape=jax.ShapeDtypeStruct((S, D), F32),
        scratch_shapes=[
            pltpu.VMEM((FFN_HALO + T, 2 * D_FF), F32),
            pltpu.VMEM((FFN_HALO, 2 * D_FF), F32),
            pltpu.VMEM((D // LANES, T, LANES), F32),
            pltpu.VMEM((T, D), F32),
        ],
        compiler_params=params,
        name="ffn",
    )(*ffn_in)
    return out.reshape(B, S, D)


def kernel(x, attn_norm, w_in, gla_wf2, gla_bf2, gla_onorm, gla_wo, lru_conv_w, lru_conv_b, lru_wa, lru_ba, lru_wx, lru_bx, lru_a_param, lru_wo, w_out, ffn_norm, ffn_wup, ffn_conv_w, ffn_conv_b, ffn_wdown, final_norm):
    return _forward(x, attn_norm, w_in, gla_wf2, gla_bf2, gla_onorm, gla_wo, lru_conv_w, lru_conv_b, lru_wa, lru_ba,
                    lru_wx, lru_bx, lru_a_param, lru_wo, w_out, ffn_norm, ffn_wup, ffn_conv_w, ffn_conv_b,
                    ffn_wdown, final_norm)
```

```python
import jax
import jax.numpy as jnp
import numpy as np
from jax import lax
from jax.experimental import pallas as pl
from jax.experimental.pallas import tpu as pltpu

D_MODEL = 1024
GLA_HEADS = 4
GLA_DK = 128
GLA_DV = 256
GLA_KEY = GLA_HEADS * GLA_DK
GLA_VAL = GLA_HEADS * GLA_DV
GLA_GATE_RANK = 16
GLA_GATE_NORM = 16.0
GLA_CHUNK = 64
LRU_WIDTH = 1344
LRU_BLOCKS = 8
LRU_BW = LRU_WIDTH // LRU_BLOCKS
LRU_CONV = 4
LRU_C = 8.0
D_FF = 3 * D_MODEL
FFN_CONV = 3
EPS = 1e-6

LANES = 128
SUBLANES = 8
MXU_DIM = 256
VMEM_LIMIT_BYTES = 58 * 1024 * 1024

LRU_PAD = ((LRU_WIDTH + LANES - 1) // LANES) * LANES
RANK_PAD = LANES
SEQ_BLOCK = 256
LRU_HALO = (LRU_CONV - 1) * SUBLANES
FFN_HALO = (FFN_CONV - 1) * SUBLANES

_IN_WIDTHS = (GLA_KEY, GLA_KEY, GLA_VAL, GLA_VAL, RANK_PAD, LRU_PAD, LRU_PAD, D_MODEL, D_MODEL)
_IN_OFFS = tuple(int(v) for v in np.cumsum((0,) + _IN_WIDTHS))
(C_Q, C_K, C_V, C_G, C_F, C_XL, C_LG, C_GA, C_GB) = [slice(_IN_OFFS[n], _IN_OFFS[n + 1]) for n in range(9)]
D_PROJ_PAD = _IN_OFFS[-1]

BF16 = jnp.bfloat16
F32 = jnp.float32


def _band_tiles():
    tiles = []
    for c0 in range(0, LRU_PAD, LANES):
        c1 = min(c0 + LANES, LRU_PAD)
        last_col = min(c1, LRU_WIDTH) - 1
        b_lo = c0 // LRU_BW
        b_hi = last_col // LRU_BW
        k0 = (b_lo * LRU_BW) // LANES * LANES
        k1 = min(-(-((b_hi + 1) * LRU_BW) // LANES) * LANES, LRU_PAD)
        tiles.append((k0, k1, c0, c1))
    return tuple(tiles)


BAND_TILES = _band_tiles()
BAND_ROWS = sum(k1 - k0 for k0, k1, _, _ in BAND_TILES)


def _dot(a, b):
    return jnp.dot(a, b, preferred_element_type=F32)


def _dot_nt(a, b):
    return lax.dot_general(a, b, (((1,), (1,)), ((), ())), preferred_element_type=F32)


def _dot_tn(a, b):
    return lax.dot_general(a, b, (((0,), (0,)), ((), ())), preferred_element_type=F32)


def _sigmoid(x):
    return 1.0 / (1.0 + jnp.exp(-x))


def _softplus(x):
    return jnp.maximum(x, 0.0) + jnp.log1p(jnp.exp(-jnp.abs(x)))


def _gelu_tanh(x):
    c = float(np.sqrt(2.0 / np.pi))
    hx = 0.5 * x
    return hx + hx * jnp.tanh(x * (c + (0.044715 * c) * (x * x)))


def _rms_norm(x, g):
    ms = jnp.mean(x * x, axis=-1, keepdims=True)
    return x * lax.rsqrt(ms + EPS) * g


def _segment_major(hb):
    T = hb.shape[0]
    seg_len = T // SUBLANES
    p = lax.broadcasted_iota(jnp.int32, (T, T), 0)
    t = lax.broadcasted_iota(jnp.int32, (T, T), 1)
    perm = jnp.where(t == (p % SUBLANES) * seg_len + p // SUBLANES, 1.0, 0.0).astype(BF16)
    return _dot(perm, hb).astype(BF16)


def _fill_halo(buf_ref, prev_ref, cur, halo, cols=slice(None)):
    T = cur.shape[0]
    sub = lax.broadcasted_iota(jnp.int32, (SUBLANES, cur.shape[1]), 0)
    for off in range(0, halo, SUBLANES):
        rows = slice(off, off + SUBLANES)
        tail = cur[T - halo + off:T - halo + off + SUBLANES, :]
        buf_ref[rows, cols] = pltpu.roll(jnp.where(sub == SUBLANES - 1, prev_ref[rows, cols], tail), 1, 0)
    prev_ref[:, cols] = cur[T - halo:T, :]


def _to_token_major(val, slab_ref, dst_ref):
    T, d = val.shape
    seg_len = T // SUBLANES
    for n in range(d // LANES):
        slab_ref[n] = val[:, n * LANES:(n + 1) * LANES]
    for j in range(T // SUBLANES):
        t0 = j * SUBLANES
        start = (t0 % seg_len) * SUBLANES + t0 // seg_len
        for n in range(d // LANES):
            dst_ref[t0:t0 + SUBLANES, n * LANES:(n + 1) * LANES] = slab_ref[n, pl.ds(start, SUBLANES, stride=SUBLANES), :]


def _mixer_kernel(x_ref, an_ref, win_ref, wf2_ref, bf2_ref, onorm_ref, gwo_ref, cw_ref, cb_ref, wband_ref, ba_ref,
                  bx_ref, ap_ref, lwo_ref, wout_ref,
                  out_ref,
                  st_ref, xl_ref, xprev_ref, hc_ref, o_ref, hl_ref, ac_ref, slab_ref, mb_ref):
    T = x_ref.shape[0]
    n_chunks = T // GLA_CHUNK
    seg_len = T // SUBLANES

    @pl.when(pl.program_id(0) == 0)
    def _():
        st_ref[...] = jnp.zeros_like(st_ref)
        xprev_ref[...] = jnp.zeros_like(xprev_ref)
        hc_ref[...] = jnp.zeros_like(hc_ref)

    x = x_ref[...]
    hb = _rms_norm(x, an_ref[...]).astype(BF16)

    f_low = _dot(hb, win_ref[:, C_F]).astype(BF16)
    z = _dot(f_low, wf2_ref[...]) + bf2_ref[...]
    hbp = _segment_major(hb)
    xl = _dot(hbp, win_ref[:, C_XL])
    xl_ref[LRU_HALO:LRU_HALO + T, :] = xl
    _fill_halo(xl_ref, xprev_ref, xl, LRU_HALO)
    gk = (jnp.minimum(z, 0.0) - jnp.log1p(jnp.exp(-jnp.abs(z)))) * (1.0 / GLA_GATE_NORM)

    row = lax.broadcasted_iota(jnp.int32, (T, T), 0)
    col = lax.broadcasted_iota(jnp.int32, (T, T), 1)
    intra = (row >= col) & ((row // GLA_CHUNK) == (col // GLA_CHUNK))
    tri = jnp.where(intra, 1.0, 0.0).astype(BF16)
    g1 = gk.astype(BF16)
    r1 = gk - g1.astype(F32)
    g2 = r1.astype(BF16)
    g3 = (r1 - g2.astype(F32)).astype(BF16)
    b = _dot(tri, g1) + _dot(tri, g2) + _dot(tri, g3)
    q = _dot(hb, win_ref[:, C_Q]) * (GLA_DK ** -0.5)
    k = _dot(hb, win_ref[:, C_K])
    v = _dot(hb, win_ref[:, C_V]).astype(BF16)

    xc = cb_ref[...] + xl * cw_ref[LRU_CONV - 1:LRU_CONV, :]
    for d in range(1, LRU_CONV):
        xc = xc + xl_ref[LRU_HALO - d * SUBLANES:LRU_HALO - d * SUBLANES + T, :] * cw_ref[LRU_CONV - 1 - d:LRU_CONV - d, :]
    xcb = xc.astype(BF16)
    row0 = 0
    r_parts, i_parts = [], []
    for (k0, k1, c0, c1) in BAND_TILES:
        both = _dot(xcb[:, k0:k1], wband_ref[row0:row0 + (k1 - k0), :])
        r_parts.append(both[:, :LANES])
        i_parts.append(both[:, LANES:])
        row0 += k1 - k0

    b_last_rows = [b[(c + 1) * GLA_CHUNK - 1:(c + 1) * GLA_CHUNK, :] for c in range(n_chunks)]
    b_last = jnp.concatenate([jnp.broadcast_to(r, (GLA_CHUNK, GLA_KEY)) for r in b_last_rows], axis=0)
    decay = [jnp.exp(r) for r in b_last_rows]
    q_t = (q * jnp.exp(b)).astype(BF16)
    k_t = (k * jnp.exp(-b)).astype(BF16)
    k_s = (k * jnp.exp(b_last - b)).astype(BF16)

    def chunk_expand(m):
        cols = []
        for c in range(n_chunks):
            parts = []
            if c > 0:
                parts.append(jnp.zeros((c * GLA_CHUNK, GLA_DK), m.dtype))
            parts.append(m[c * GLA_CHUNK:(c + 1) * GLA_CHUNK, :])
            if c < n_chunks - 1:
                parts.append(jnp.zeros((T - (c + 1) * GLA_CHUNK, GLA_DK), m.dtype))
            cols.append(jnp.concatenate(parts, axis=0))
        return jnp.concatenate(cols, axis=1)

    for hh in range(GLA_HEADS):
        ks_ = slice(hh * GLA_DK, (hh + 1) * GLA_DK)
        vs_ = slice(hh * GLA_DV, (hh + 1) * GLA_DV)
        qh = q_t[:, ks_]
        vh = v[:, vs_]
        a = jnp.where(intra, _dot_nt(qh, k_t[:, ks_]), 0.0).astype(BF16)
        ds_all = _dot_tn(vh, chunk_expand(k_s[:, ks_]))
        st = st_ref[hh]
        starts = []
        for c in range(n_chunks):
            starts.append(st.astype(BF16))
            st = st * decay[c][:, ks_] + ds_all[:, c * GLA_DK:(c + 1) * GLA_DK]
        st_ref[hh] = st
        o_ref[:, vs_] = _dot(a, vh) + _dot_nt(chunk_expand(qh), jnp.concatenate(starts, axis=1))

    g = _dot(hb, win_ref[:, C_G])
    lg = _dot(hbp, win_ref[:, C_LG])

    r_gate = _sigmoid(jnp.concatenate(r_parts, axis=1) + ba_ref[...])
    i_gate = _sigmoid(jnp.concatenate(i_parts, axis=1) + bx_ref[...])
    log_a = (-LRU_C) * r_gate * _softplus(ap_ref[...])
    a_s = jnp.exp(log_a)
    u_s = jnp.sqrt(-jnp.tanh(log_a) * (1.0 + a_s * a_s)) * (i_gate * xc)

    h_loc = u_s[0:SUBLANES, :]
    a_cum = a_s[0:SUBLANES, :]
    hl_ref[0:SUBLANES, :] = h_loc
    ac_ref[0:SUBLANES, :] = a_cum
    for r in range(1, seg_len):
        rs = slice(r * SUBLANES, (r + 1) * SUBLANES)
        h_loc = a_s[rs, :] * h_loc + u_s[rs, :]
        a_cum = a_s[rs, :] * a_cum
        hl_ref[rs, :] = h_loc
        ac_ref[rs, :] = a_cum
    sub = lax.broadcasted_iota(jnp.int32, (SUBLANES, LRU_PAD), 0)
    e_s, p_s = h_loc, a_cum
    s = 1
    while s < SUBLANES:
        m = sub >= s
        e_s = jnp.where(m, e_s + p_s * pltpu.roll(e_s, s, 0), e_s)
        p_s = jnp.where(m, p_s * pltpu.roll(p_s, s, 0), p_s)
        s *= 2
    h_carry = hc_ref[...]
    seg_end = e_s + p_s * h_carry
    seg_start = jnp.where(sub == 0, h_carry, pltpu.roll(seg_end, 1, 0))
    hc_ref[...] = seg_end[SUBLANES - 1:SUBLANES, :]
    for r in range(seg_len):
        rs = slice(r * SUBLANES, (r + 1) * SUBLANES)
        hl_ref[rs, :] = hl_ref[rs, :] + ac_ref[rs, :] * seg_start

    gb = _dot(hbp, win_ref[:, C_GB])
    ga = _dot(hb, win_ref[:, C_GA])

    og = []
    for hh in range(GLA_HEADS):
        vs_ = slice(hh * GLA_DV, (hh + 1) * GLA_DV)
        gh = g[:, vs_]
        og.append(_rms_norm(o_ref[:, vs_], onorm_ref[...]) * (gh * _sigmoid(gh)))
    y_a = _dot(jnp.concatenate(og, axis=1).astype(BF16), gwo_ref[...])

    y_b = _dot((hl_ref[...] * _gelu_tanh(lg)).astype(BF16), lwo_ref[...])
    gated_a = _sigmoid(ga) * y_a
    _to_token_major(_sigmoid(gb) * y_b, slab_ref, mb_ref)

    merged = (gated_a + mb_ref[...]).astype(BF16)
    out_ref[...] = x + _dot(merged, wout_ref[...])


def _ffn_kernel(x_ref, fn_ref, wup_ref, cw_ref, cb_ref, wdown_ref, final_ref, out_ref,
                u_ref, uprev_ref, slab_ref, y_ref):
    T = x_ref.shape[0]

    @pl.when(pl.program_id(0) == 0)
    def _():
        uprev_ref[...] = jnp.zeros_like(uprev_ref)

    x = x_ref[...]
    hbp = _segment_major(_rms_norm(x, fn_ref[...]).astype(BF16))
    u = _dot(hbp, wup_ref[...])
    u_ref[FFN_HALO:FFN_HALO + T, :] = u
    _fill_halo(u_ref, uprev_ref, u, FFN_HALO)
    uc = cb_ref[...] + u * cw_ref[FFN_CONV - 1:FFN_CONV, :]
    for d in range(1, FFN_CONV):
        rows = slice(FFN_HALO - d * SUBLANES, FFN_HALO - d * SUBLANES + T)
        uc = uc + u_ref[rows, :] * cw_ref[FFN_CONV - 1 - d:FFN_CONV - d, :]
    act = (_gelu_tanh(uc[:, :D_FF]) * uc[:, D_FF:]).astype(BF16)
    _to_token_major(_dot(act, wdown_ref[...]), slab_ref, y_ref)
    out_ref[...] = _rms_norm(x + y_ref[...], final_ref[...])


def _const_spec(shape):
    nd = len(shape)
    return pl.BlockSpec(shape, lambda i, _nd=nd: (0,) * _nd, pipeline_mode=pl.Buffered(1))


def _pad_cols(w, n):
    return jnp.pad(w, ((0, 0), (0, n - w.shape[1])))


def _band_pack(w_a, w_x):
    w = jnp.stack([w_a, w_x])
    eye = jnp.eye(LRU_BLOCKS, dtype=w.dtype)
    dense = (w[:, :, :, None, :] * eye[None, :, None, :, None]).reshape(2, LRU_WIDTH, LRU_WIDTH)
    dense = jnp.pad(dense, ((0, 0), (0, LRU_PAD - LRU_WIDTH), (0, LRU_PAD - LRU_WIDTH)))
    tiles = [jnp.concatenate([dense[0, k0:k1, c0:c1], dense[1, k0:k1, c0:c1]], axis=1)
             for (k0, k1, c0, c1) in BAND_TILES]
    return jnp.concatenate(tiles, axis=0).astype(BF16)


def _row(v, n=None):
    v = v.reshape(1, -1).astype(F32)
    return v if n is None else _pad_cols(v, n)


@jax.jit
def _forward(x, attn_norm, w_in, gla_wf2, gla_bf2, gla_onorm, gla_wo, lru_conv_w, lru_conv_b, lru_wa, lru_ba,
             lru_wx, lru_bx, lru_a_param, lru_wo, w_out, ffn_norm, ffn_wup, ffn_conv_w, ffn_conv_b, ffn_wdown,
             final_norm):
    B, S, D = x.shape
    assert B == 1 and D == D_MODEL and S % SEQ_BLOCK == 0 and attn_norm.shape[0] == 1
    T = SEQ_BLOCK
    x2 = x.reshape(S, D)

    splits = np.cumsum([0, GLA_KEY, GLA_KEY, GLA_VAL, GLA_VAL, GLA_GATE_RANK, LRU_WIDTH, LRU_WIDTH, D_MODEL, D_MODEL])
    wi = w_in[0]
    w_in_packed = jnp.concatenate(
        [_pad_cols(wi[:, splits[n]:splits[n + 1]], _IN_WIDTHS[n]) for n in range(9)], axis=1).astype(BF16)
    wf2 = jnp.pad(gla_wf2[0], ((0, RANK_PAD - GLA_GATE_RANK), (0, 0)))
    lwo = jnp.pad(lru_wo[0], ((0, LRU_PAD - LRU_WIDTH), (0, 0)))

    mixer_in = [
        x2, _row(attn_norm[0]), w_in_packed,
        wf2.astype(BF16), _row(gla_bf2[0]), _row(gla_onorm[0]), gla_wo[0].astype(BF16),
        _pad_cols(lru_conv_w[0].astype(F32), LRU_PAD), _row(lru_conv_b[0], LRU_PAD),
        _band_pack(lru_wa[0], lru_wx[0]), _row(lru_ba[0], LRU_PAD), _row(lru_bx[0], LRU_PAD),
        _row(lru_a_param[0], LRU_PAD), lwo.astype(BF16), w_out[0].astype(BF16),
    ]
    blk = pl.BlockSpec((T, D), lambda i: (i, 0))
    params = pltpu.CompilerParams(dimension_semantics=("arbitrary",), vmem_limit_bytes=VMEM_LIMIT_BYTES)
    x1 = pl.pallas_call(
        _mixer_kernel,
        grid=(S // T,),
        in_specs=[blk] + [_const_spec(a.shape) for a in mixer_in[1:]],
        out_specs=blk,
        out_shape=jax.ShapeDtypeStruct((S, D), F32),
        scratch_shapes=[
            pltpu.VMEM((GLA_HEADS, GLA_DV, GLA_DK), F32),
            pltpu.VMEM((LRU_HALO + T, LRU_PAD), F32),
            pltpu.VMEM((LRU_HALO, LRU_PAD), F32),
            pltpu.VMEM((1, LRU_PAD), F32),
            pltpu.VMEM((T, GLA_VAL), F32),
            pltpu.VMEM((T, LRU_PAD), F32),
            pltpu.VMEM((T, LRU_PAD), F32),
            pltpu.VMEM((D // LANES, T, LANES), F32),
            pltpu.VMEM((T, D), F32),
        ],
        compiler_params=params,
        name="mixer",
    )(*mixer_in)

    ffn_in = [
        x1, _row(ffn_norm[0]), ffn_wup[0].astype(BF16), ffn_conv_w[0].astype(F32), _row(ffn_conv_b[0]),
        ffn_wdown[0].astype(BF16), _row(final_norm),
    ]
    out = pl.pallas_call(
        _ffn_kernel,
        grid=(S // T,),
        in_specs=[blk] + [_const_spec(a.shape) for a in ffn_in[1:]],
        out_specs=blk,
        out_shape=jax.ShapeDtypeStruct((S, D), F32),
        scratch_shapes=[
            pltpu.VMEM((FFN_HALO + T, 2 * D_FF), F32),
            pltpu.VMEM((FFN_HALO, 2 * D_FF), F32),
            pltpu.VMEM((D // LANES, T, LANES), F32),
            pltpu.VMEM((T, D), F32),
        ],
        compiler_params=params,
        name="ffn",
    )(*ffn_in)
    return out.reshape(B, S, D)


def kernel(x, attn_norm, w_in, gla_wf2, gla_bf2, gla_onorm, gla_wo, lru_conv_w, lru_conv_b, lru_wa, lru_ba, lru_wx, lru_bx, lru_a_param, lru_wo, w_out, ffn_norm, ffn_wup, ffn_conv_w, ffn_conv_b, ffn_wdown, final_norm):
    return _forward(x, attn_norm, w_in, gla_wf2, gla_bf2, gla_onorm, gla_wo, lru_conv_w, lru_conv_b, lru_wa, lru_ba,
                    lru_wx, lru_bx, lru_a_param, lru_wo, w_out, ffn_norm, ffn_wup, ffn_conv_w, ffn_conv_b,
                    ffn_wdown, final_norm)
```

```python
import jax
import jax.numpy as jnp
import numpy as np
from jax import lax
from jax.experimental import pallas as pl
from jax.experimental.pallas import tpu as pltpu

D_MODEL = 1024
GLA_HEADS = 4
GLA_DK = 128
GLA_DV = 256
GLA_KEY = GLA_HEADS * GLA_DK
GLA_VAL = GLA_HEADS * GLA_DV
GLA_GATE_RANK = 16
GLA_GATE_NORM = 16.0
GLA_CHUNK = 64
LRU_WIDTH = 1344
LRU_BLOCKS = 8
LRU_BW = LRU_WIDTH // LRU_BLOCKS
LRU_CONV = 4
LRU_C = 8.0
D_FF = 3 * D_MODEL
FFN_CONV = 3
EPS = 1e-6

LANES = 128
SUBLANES = 8
MXU_DIM = 256
VMEM_LIMIT_BYTES = 58 * 1024 * 1024

LRU_PAD = ((LRU_WIDTH + LANES - 1) // LANES) * LANES
RANK_PAD = LANES
SEQ_BLOCK = 256
LRU_HALO = (LRU_CONV - 1) * SUBLANES
FFN_HALO = (FFN_CONV - 1) * SUBLANES

(O_Q, O_K, O_V, O_G, O_F, O_XL, O_LG, O_GA, O_GB, O_END) = [int(v) for v in np.cumsum(
    [0, GLA_KEY, GLA_KEY, GLA_VAL, GLA_VAL, GLA_GATE_RANK, LRU_WIDTH, LRU_WIDTH, D_MODEL, D_MODEL])]

(V_ATTN_NORM, V_GLA_BF2, V_GLA_ONORM, V_LRU_CONV_B, V_LRU_BA, V_LRU_BX, V_LRU_A_PARAM, V_LRU_CONV_W) = range(8)
(V_FFN_NORM, V_FINAL_NORM, V_FFN_CONV_B, V_FFN_CONV_W) = range(4)

BF16 = jnp.bfloat16
F32 = jnp.float32


def _band_tiles():
    tiles = []
    for c0 in range(0, LRU_PAD, LANES):
        c1 = min(c0 + LANES, LRU_PAD)
        last_col = min(c1, LRU_WIDTH) - 1
        b_lo = c0 // LRU_BW
        b_hi = last_col // LRU_BW
        k0 = (b_lo * LRU_BW) // LANES * LANES
        k1 = min(-(-((b_hi + 1) * LRU_BW) // LANES) * LANES, LRU_PAD)
        tiles.append((k0, k1, c0, c1))
    return tuple(tiles)


BAND_TILES = _band_tiles()
BAND_ROWS = sum(k1 - k0 for k0, k1, _, _ in BAND_TILES)


def _dot(a, b):
    return jnp.dot(a, b, preferred_element_type=F32)


def _dot_nt(a, b):
    return lax.dot_general(a, b, (((1,), (1,)), ((), ())), preferred_element_type=F32)


def _dot_tn(a, b):
    return lax.dot_general(a, b, (((0,), (0,)), ((), ())), preferred_element_type=F32)


def _sigmoid(x):
    return 1.0 / (1.0 + jnp.exp(-x))


def _softplus(x):
    return jnp.maximum(x, 0.0) + jnp.log1p(jnp.exp(-jnp.abs(x)))


def _gelu_tanh(x):
    c = float(np.sqrt(2.0 / np.pi))
    hx = 0.5 * x
    return hx + hx * jnp.tanh(x * (c + (0.044715 * c) * (x * x)))


def _rms_norm(x, g):
    ms = jnp.mean(x * x, axis=-1, keepdims=True)
    return x * lax.rsqrt(ms + EPS) * g


def _segment_major(hb):
    T = hb.shape[0]
    seg_len = T // SUBLANES
    p = lax.broadcasted_iota(jnp.int32, (T, T), 0)
    t = lax.broadcasted_iota(jnp.int32, (T, T), 1)
    perm = jnp.where(t == (p % SUBLANES) * seg_len + p // SUBLANES, 1.0, 0.0).astype(BF16)
    return _dot(perm, hb).astype(BF16)


def _fill_halo(buf_ref, prev_ref, cur, halo, cols=slice(None)):
    T = cur.shape[0]
    sub = lax.broadcasted_iota(jnp.int32, (SUBLANES, cur.shape[1]), 0)
    for off in range(0, halo, SUBLANES):
        rows = slice(off, off + SUBLANES)
        tail = cur[T - halo + off:T - halo + off + SUBLANES, :]
        buf_ref[rows, cols] = pltpu.roll(jnp.where(sub == SUBLANES - 1, prev_ref[rows, cols], tail), 1, 0)
    prev_ref[:, cols] = cur[T - halo:T, :]


def _to_token_major(val, slab_ref, dst_ref):
    T, d = val.shape
    seg_len = T // SUBLANES
    for n in range(d // LANES):
        slab_ref[n] = val[:, n * LANES:(n + 1) * LANES]
    for j in range(T // SUBLANES):
        t0 = j * SUBLANES
        start = (t0 % seg_len) * SUBLANES + t0 // seg_len
        for n in range(d // LANES):
            dst_ref[t0:t0 + SUBLANES, n * LANES:(n + 1) * LANES] = slab_ref[n, pl.ds(start, SUBLANES, stride=SUBLANES), :]


def _build_band(w_refs, wband_ref):
    src = lax.broadcasted_iota(jnp.int32, (MXU_DIM, LANES), 0)
    dst = lax.broadcasted_iota(jnp.int32, (MXU_DIM, LANES), 1)
    row0 = 0
    for (k0, k1, c0, c1) in BAND_TILES:
        row_block = (lax.broadcasted_iota(jnp.int32, (k1 - k0, MXU_DIM), 0) + k0) // LRU_BW
        for gate, w_ref in enumerate(w_refs):
            rows = w_ref[k0:k1, :]
            tile = jnp.zeros((k1 - k0, LANES), F32)
            for n in range(c0 // LRU_BW, min((c1 - 1) // LRU_BW, LRU_BLOCKS - 1) + 1):
                place = jnp.where(src == dst + (c0 - n * LRU_BW), 1.0, 0.0).astype(BF16)
                tile = tile + _dot(jnp.where(row_block == n, rows, 0.0).astype(BF16), place)
            wband_ref[row0:row0 + (k1 - k0), gate * LANES:(gate + 1) * LANES] = tile.astype(BF16)
        row0 += k1 - k0


def _mixer_kernel(x_ref, vec_ref, wqkvg_ref, wf_ref, wxl_ref, wlg_ref, wgab_ref, wf2_ref, gwo_ref,
                  wa_ref, wx_ref, lwo_ref, wout_ref,
                  out_ref,
                  st_ref, xl_ref, xprev_ref, hc_ref, o_ref, hl_ref, ac_ref, slab_ref, mb_ref, wband_ref):
    T = x_ref.shape[0]
    n_chunks = T // GLA_CHUNK
    seg_len = T // SUBLANES

    @pl.when(pl.program_id(0) == 0)
    def _():
        st_ref[...] = jnp.zeros_like(st_ref)
        xprev_ref[...] = jnp.zeros_like(xprev_ref)
        hc_ref[...] = jnp.zeros_like(hc_ref)
        _build_band((wa_ref, wx_ref), wband_ref)

    def vec(r, n=LRU_PAD):
        return vec_ref[r:r + 1, :n]

    x = x_ref[...]
    hb = _rms_norm(x, vec(V_ATTN_NORM, D_MODEL)).astype(BF16)

    f_low = _dot(hb, wf_ref[...]).astype(BF16)
    z = _dot(f_low, wf2_ref[...]) + vec(V_GLA_BF2, GLA_KEY)
    hbp = _segment_major(hb)
    xl = _dot(hbp, wxl_ref[...])
    xl_ref[LRU_HALO:LRU_HALO + T, :] = xl
    _fill_halo(xl_ref, xprev_ref, xl, LRU_HALO)
    gk = (jnp.minimum(z, 0.0) - jnp.log1p(jnp.exp(-jnp.abs(z)))) * (1.0 / GLA_GATE_NORM)

    row = lax.broadcasted_iota(jnp.int32, (T, T), 0)
    col = lax.broadcasted_iota(jnp.int32, (T, T), 1)
    intra = (row >= col) & ((row // GLA_CHUNK) == (col // GLA_CHUNK))
    tri = jnp.where(intra, 1.0, 0.0).astype(BF16)
    g1 = gk.astype(BF16)
    r1 = gk - g1.astype(F32)
    g2 = r1.astype(BF16)
    g3 = (r1 - g2.astype(F32)).astype(BF16)
    b = _dot(tri, g1) + _dot(tri, g2) + _dot(tri, g3)
    q = _dot(hb, wqkvg_ref[:, O_Q:O_K]) * (GLA_DK ** -0.5)
    k = _dot(hb, wqkvg_ref[:, O_K:O_V])
    v = _dot(hb, wqkvg_ref[:, O_V:O_G]).astype(BF16)

    xc = vec(V_LRU_CONV_B) + xl * vec(V_LRU_CONV_W + LRU_CONV - 1)
    for d in range(1, LRU_CONV):
        rows = slice(LRU_HALO - d * SUBLANES, LRU_HALO - d * SUBLANES + T)
        xc = xc + xl_ref[rows, :] * vec(V_LRU_CONV_W + LRU_CONV - 1 - d)
    xcb = xc.astype(BF16)
    row0 = 0
    r_parts, i_parts = [], []
    for (k0, k1, c0, c1) in BAND_TILES:
        both = _dot(xcb[:, k0:k1], wband_ref[row0:row0 + (k1 - k0), :])
        r_parts.append(both[:, :LANES])
        i_parts.append(both[:, LANES:])
        row0 += k1 - k0

    b_last_rows = [b[(c + 1) * GLA_CHUNK - 1:(c + 1) * GLA_CHUNK, :] for c in range(n_chunks)]
    b_last = jnp.concatenate([jnp.broadcast_to(r, (GLA_CHUNK, GLA_KEY)) for r in b_last_rows], axis=0)
    decay = [jnp.exp(r) for r in b_last_rows]
    q_t = (q * jnp.exp(b)).astype(BF16)
    k_t = (k * jnp.exp(-b)).astype(BF16)
    k_s = (k * jnp.exp(b_last - b)).astype(BF16)

    def chunk_expand(m):
        cols = []
        for c in range(n_chunks):
            parts = []
            if c > 0:
                parts.append(jnp.zeros((c * GLA_CHUNK, GLA_DK), m.dtype))
            parts.append(m[c * GLA_CHUNK:(c + 1) * GLA_CHUNK, :])
            if c < n_chunks - 1:
                parts.append(jnp.zeros((T - (c + 1) * GLA_CHUNK, GLA_DK), m.dtype))
            cols.append(jnp.concatenate(parts, axis=0))
        return jnp.concatenate(cols, axis=1)

    for hh in range(GLA_HEADS):
        ks_ = slice(hh * GLA_DK, (hh + 1) * GLA_DK)
        vs_ = slice(hh * GLA_DV, (hh + 1) * GLA_DV)
        qh = q_t[:, ks_]
        vh = v[:, vs_]
        a = jnp.where(intra, _dot_nt(qh, k_t[:, ks_]), 0.0).astype(BF16)
        ds_all = _dot_tn(vh, chunk_expand(k_s[:, ks_]))
        st = st_ref[hh]
        starts = []
        for c in range(n_chunks):
            starts.append(st.astype(BF16))
            st = st * decay[c][:, ks_] + ds_all[:, c * GLA_DK:(c + 1) * GLA_DK]
        st_ref[hh] = st
        o_ref[:, vs_] = _dot(a, vh) + _dot_nt(chunk_expand(qh), jnp.concatenate(starts, axis=1))

    g = _dot(hb, wqkvg_ref[:, O_G:O_F])
    lg = _dot(hbp, wlg_ref[...])

    r_gate = _sigmoid(jnp.concatenate(r_parts, axis=1) + vec(V_LRU_BA))
    i_gate = _sigmoid(jnp.concatenate(i_parts, axis=1) + vec(V_LRU_BX))
    log_a = (-LRU_C) * r_gate * _softplus(vec(V_LRU_A_PARAM))
    a_s = jnp.exp(log_a)
    u_s = jnp.sqrt(-jnp.tanh(log_a) * (1.0 + a_s * a_s)) * (i_gate * xc)

    h_loc = u_s[0:SUBLANES, :]
    a_cum = a_s[0:SUBLANES, :]
    hl_ref[0:SUBLANES, :] = h_loc
    ac_ref[0:SUBLANES, :] = a_cum
    for r in range(1, seg_len):
        rs = slice(r * SUBLANES, (r + 1) * SUBLANES)
        h_loc = a_s[rs, :] * h_loc + u_s[rs, :]
        a_cum = a_s[rs, :] * a_cum
        hl_ref[rs, :] = h_loc
        ac_ref[rs, :] = a_cum
    sub = lax.broadcasted_iota(jnp.int32, (SUBLANES, LRU_PAD), 0)
    e_s, p_s = h_loc, a_cum
    s = 1
    while s < SUBLANES:
        m = sub >= s
        e_s = jnp.where(m, e_s + p_s * pltpu.roll(e_s, s, 0), e_s)
        p_s = jnp.where(m, p_s * pltpu.roll(p_s, s, 0), p_s)
        s *= 2
    h_carry = hc_ref[...]
    seg_end = e_s + p_s * h_carry
    seg_start = jnp.where(sub == 0, h_carry, pltpu.roll(seg_end, 1, 0))
    hc_ref[...] = seg_end[SUBLANES - 1:SUBLANES, :]
    for r in range(seg_len):
        rs = slice(r * SUBLANES, (r + 1) * SUBLANES)
        hl_ref[rs, :] = hl_ref[rs, :] + ac_ref[rs, :] * seg_start

    gb = _dot(hbp, wgab_ref[:, D_MODEL:])
    ga = _dot(hb, wgab_ref[:, :D_MODEL])

    og = []
    for hh in range(GLA_HEADS):
        vs_ = slice(hh * GLA_DV, (hh + 1) * GLA_DV)
        gh = g[:, vs_]
        og.append(_rms_norm(o_ref[:, vs_], vec(V_GLA_ONORM, GLA_DV)) * (gh * _sigmoid(gh)))
    y_a = _dot(jnp.concatenate(og, axis=1).astype(BF16), gwo_ref[...])

    y_b = _dot((hl_ref[...] * _gelu_tanh(lg)).astype(BF16), lwo_ref[...])
    gated_a = _sigmoid(ga) * y_a
    _to_token_major(_sigmoid(gb) * y_b, slab_ref, mb_ref)

    merged = (gated_a + mb_ref[...]).astype(BF16)
    out_ref[...] = x + _dot(merged, wout_ref[...])


def _ffn_kernel(x_ref, vec_ref, wup_ref, wdown_ref, out_ref,
                u_ref, uprev_ref, slab_ref, y_ref):
    T = x_ref.shape[0]

    @pl.when(pl.program_id(0) == 0)
    def _():
        uprev_ref[...] = jnp.zeros_like(uprev_ref)

    def vec(r, n=2 * D_FF):
        return vec_ref[r:r + 1, :n]

    x = x_ref[...]
    hbp = _segment_major(_rms_norm(x, vec(V_FFN_NORM, D_MODEL)).astype(BF16))
    u = _dot(hbp, wup_ref[...])
    u_ref[FFN_HALO:FFN_HALO + T, :] = u
    _fill_halo(u_ref, uprev_ref, u, FFN_HALO)
    uc = vec(V_FFN_CONV_B) + u * vec(V_FFN_CONV_W + FFN_CONV - 1)
    for d in range(1, FFN_CONV):
        rows = slice(FFN_HALO - d * SUBLANES, FFN_HALO - d * SUBLANES + T)
        uc = uc + u_ref[rows, :] * vec(V_FFN_CONV_W + FFN_CONV - 1 - d)
    act = (_gelu_tanh(uc[:, :D_FF]) * uc[:, D_FF:]).astype(BF16)
    _to_token_major(_dot(act, wdown_ref[...]), slab_ref, y_ref)
    out_ref[...] = _rms_norm(x + y_ref[...], vec(V_FINAL_NORM, D_MODEL))


def _const_spec(shape):
    nd = len(shape)
    return pl.BlockSpec(shape, lambda i, _nd=nd: (0,) * _nd, pipeline_mode=pl.Buffered(1))


def _pad_cols(w, n):
    return jnp.pad(w, ((0, 0), (0, n - w.shape[1])))


def _block_rows(w):
    rows = w.reshape(LRU_WIDTH, LRU_BW).astype(F32)
    return jnp.pad(rows, ((0, LRU_PAD - LRU_WIDTH), (0, MXU_DIM - LRU_BW)))


def _pack_rows(vectors, n):
    return jnp.concatenate([_pad_cols(v.reshape(-1, v.shape[-1]).astype(F32), n) for v in vectors], axis=0)


@jax.jit
def _forward(x, attn_norm, w_in, gla_wf2, gla_bf2, gla_onorm, gla_wo, lru_conv_w, lru_conv_b, lru_wa, lru_ba,
             lru_wx, lru_bx, lru_a_param, lru_wo, w_out, ffn_norm, ffn_wup, ffn_conv_w, ffn_conv_b, ffn_wdown,
             final_norm):
    B, S, D = x.shape
    assert B == 1 and D == D_MODEL and S % SEQ_BLOCK == 0 and attn_norm.shape[0] == 1
    T = SEQ_BLOCK
    x2 = x.reshape(S, D)

    wi = w_in[0]
    w_qkvg = wi[:, O_Q:O_F].astype(BF16)
    w_f = _pad_cols(wi[:, O_F:O_XL], RANK_PAD).astype(BF16)
    w_xl = wi[:, O_XL:O_XL + LRU_PAD].astype(BF16)
    w_lg = wi[:, O_LG:O_LG + LRU_PAD].astype(BF16)
    w_gab = wi[:, O_GA:O_END].astype(BF16)
    wf2 = jnp.pad(gla_wf2[0], ((0, RANK_PAD - GLA_GATE_RANK), (0, 0)))
    lwo = jnp.pad(lru_wo[0], ((0, LRU_PAD - LRU_WIDTH), (0, 0)))

    mixer_vec = _pack_rows([attn_norm[0], gla_bf2[0], gla_onorm[0], lru_conv_b[0], lru_ba[0], lru_bx[0],
                            lru_a_param[0], lru_conv_w[0]], LRU_PAD)
    mixer_in = [
        x2, mixer_vec, w_qkvg, w_f, w_xl, w_lg, w_gab, wf2.astype(BF16), gla_wo[0].astype(BF16),
        _block_rows(lru_wa[0]), _block_rows(lru_wx[0]), lwo.astype(BF16), w_out[0].astype(BF16),
    ]
    blk = pl.BlockSpec((T, D), lambda i: (i, 0))
    params = pltpu.CompilerParams(dimension_semantics=("arbitrary",), vmem_limit_bytes=VMEM_LIMIT_BYTES)
    x1 = pl.pallas_call(
        _mixer_kernel,
        grid=(S // T,),
        in_specs=[blk] + [_const_spec(a.shape) for a in mixer_in[1:]],
        out_specs=blk,
        out_shape=jax.ShapeDtypeStruct((S, D), F32),
        scratch_shapes=[
            pltpu.VMEM((GLA_HEADS, GLA_DV, GLA_DK), F32),
            pltpu.VMEM((LRU_HALO + T, LRU_PAD), F32),
            pltpu.VMEM((LRU_HALO, LRU_PAD), F32),
            pltpu.VMEM((1, LRU_PAD), F32),
            pltpu.VMEM((T, GLA_VAL), F32),
            pltpu.VMEM((T, LRU_PAD), F32),
            pltpu.VMEM((T, LRU_PAD), F32),
            pltpu.VMEM((D // LANES, T, LANES), F32),
            pltpu.VMEM((T, D), F32),
            pltpu.VMEM((BAND_ROWS, 2 * LANES), BF16),
        ],
        compiler_params=params,
        name="mixer",
    )(*mixer_in)

    ffn_vec = _pack_rows([ffn_norm[0], final_norm, ffn_conv_b[0], ffn_conv_w[0]], 2 * D_FF)
    ffn_in = [x1, ffn_vec, ffn_wup[0].astype(BF16), ffn_wdown[0].astype(BF16)]
    out = pl.pallas_call(
        _ffn_kernel,
        grid=(S // T,),
        in_specs=[blk] + [_const_spec(a.shape) for a in ffn_in[1:]],
        out_specs=blk,
        out_shape=jax.ShapeDtypeStruct((S, D), F32),
        scratch_shapes=[
            pltpu.VMEM((FFN_HALO + T, 2 * D_FF), F32),
            pltpu.VMEM((FFN_HALO, 2 * D_FF), F32),
            pltpu.VMEM((D // LANES, T, LANES), F32),
            pltpu.VMEM((T, D), F32),
        ],
        compiler_params=params,
        name="ffn",
    )(*ffn_in)
    return out.reshape(B, S, D)


def kernel(x, attn_norm, w_in, gla_wf2, gla_bf2, gla_onorm, gla_wo, lru_conv_w, lru_conv_b, lru_wa, lru_ba, lru_wx, lru_bx, lru_a_param, lru_wo, w_out, ffn_norm, ffn_wup, ffn_conv_w, ffn_conv_b, ffn_wdown, final_norm):
    return _forward(x, attn_norm, w_in, gla_wf2, gla_bf2, gla_onorm, gla_wo, lru_conv_w, lru_conv_b, lru_wa, lru_ba,
                    lru_wx, lru_bx, lru_a_param, lru_wo, w_out, ffn_norm, ffn_wup, ffn_conv_w, ffn_conv_b,
                    ffn_wdown, final_norm)
```

```python
import jax
import jax.numpy as jnp
import numpy as np
from jax import lax
from jax.experimental import pallas as pl
from jax.experimental.pallas import tpu as pltpu

D_MODEL = 1024
GLA_HEADS = 4
GLA_DK = 128
GLA_DV = 256
GLA_KEY = GLA_HEADS * GLA_DK
GLA_VAL = GLA_HEADS * GLA_DV
GLA_GATE_RANK = 16
GLA_GATE_NORM = 16.0
GLA_CHUNK = 64
LRU_WIDTH = 1344
LRU_BLOCKS = 8
LRU_BW = LRU_WIDTH // LRU_BLOCKS
LRU_CONV = 4
LRU_C = 8.0
D_FF = 3 * D_MODEL
FFN_CONV = 3
EPS = 1e-6

LANES = 128
SUBLANES = 8
MXU_DIM = 256
VMEM_LIMIT_BYTES = 58 * 1024 * 1024

LRU_PAD = ((LRU_WIDTH + LANES - 1) // LANES) * LANES
RANK_PAD = LANES
SEQ_BLOCK = 256
LRU_HALO = (LRU_CONV - 1) * SUBLANES
FFN_HALO = (FFN_CONV - 1) * SUBLANES
FFN_SUB = 2
MIX_SUB = 1

(O_Q, O_K, O_V, O_G, O_F, O_XL, O_LG, O_GA, O_GB, O_END) = [int(v) for v in np.cumsum(
    [0, GLA_KEY, GLA_KEY, GLA_VAL, GLA_VAL, GLA_GATE_RANK, LRU_WIDTH, LRU_WIDTH, D_MODEL, D_MODEL])]

(V_ATTN_NORM, V_GLA_BF2, V_GLA_ONORM, V_LRU_CONV_B, V_LRU_BA, V_LRU_BX, V_LRU_A_PARAM, V_LRU_CONV_W) = range(8)
(V_FFN_NORM, V_FINAL_NORM, V_FFN_CONV_B, V_FFN_CONV_W) = range(4)

BF16 = jnp.bfloat16
F32 = jnp.float32


def _band_tiles():
    tiles = []
    for c0 in range(0, LRU_PAD, LANES):
        c1 = min(c0 + LANES, LRU_PAD)
        last_col = min(c1, LRU_WIDTH) - 1
        b_lo = c0 // LRU_BW
        b_hi = last_col // LRU_BW
        k0 = (b_lo * LRU_BW) // LANES * LANES
        k1 = min(-(-((b_hi + 1) * LRU_BW) // LANES) * LANES, LRU_PAD)
        tiles.append((k0, k1, c0, c1))
    return tuple(tiles)


BAND_TILES = _band_tiles()
BAND_ROWS = sum(k1 - k0 for k0, k1, _, _ in BAND_TILES)


def _dot(a, b):
    return jnp.dot(a, b, preferred_element_type=F32)


def _dot_nt(a, b):
    return lax.dot_general(a, b, (((1,), (1,)), ((), ())), preferred_element_type=F32)


def _dot_tn(a, b):
    return lax.dot_general(a, b, (((0,), (0,)), ((), ())), preferred_element_type=F32)


def _sigmoid(x):
    return 1.0 / (1.0 + jnp.exp(-x))


def _softplus(x):
    return jnp.maximum(x, 0.0) + jnp.log1p(jnp.exp(-jnp.abs(x)))


def _gelu_tanh(x):
    c = float(np.sqrt(2.0 / np.pi))
    hx = 0.5 * x
    return hx + hx * jnp.tanh(x * (c + (0.044715 * c) * (x * x)))


def _rms_norm(x, g):
    ms = jnp.mean(x * x, axis=-1, keepdims=True)
    return x * lax.rsqrt(ms + EPS) * g


def _segment_major(hb):
    T = hb.shape[0]
    seg_len = T // SUBLANES
    p = lax.broadcasted_iota(jnp.int32, (T, T), 0)
    t = lax.broadcasted_iota(jnp.int32, (T, T), 1)
    perm = jnp.where(t == (p % SUBLANES) * seg_len + p // SUBLANES, 1.0, 0.0).astype(BF16)
    return _dot(perm, hb).astype(BF16)


def _fill_halo(buf_ref, prev_ref, cur, halo, cols=slice(None)):
    T = cur.shape[0]
    sub = lax.broadcasted_iota(jnp.int32, (SUBLANES, cur.shape[1]), 0)
    for off in range(0, halo, SUBLANES):
        rows = slice(off, off + SUBLANES)
        tail = cur[T - halo + off:T - halo + off + SUBLANES, :]
        buf_ref[rows, cols] = pltpu.roll(jnp.where(sub == SUBLANES - 1, prev_ref[rows, cols], tail), 1, 0)
    prev_ref[:, cols] = cur[T - halo:T, :]


def _to_token_major(val, slab_ref, dst_ref, row0=0):
    T, d = val.shape
    seg_len = T // SUBLANES
    for n in range(d // LANES):
        slab_ref[n] = val[:, n * LANES:(n + 1) * LANES]
    for j in range(T // SUBLANES):
        t0 = j * SUBLANES
        start = (t0 % seg_len) * SUBLANES + t0 // seg_len
        for n in range(d // LANES):
            dst_ref[row0 + t0:row0 + t0 + SUBLANES, n * LANES:(n + 1) * LANES] = (
                slab_ref[n, pl.ds(start, SUBLANES, stride=SUBLANES), :])


def _build_band(w_refs, wband_ref):
    src = lax.broadcasted_iota(jnp.int32, (MXU_DIM, LANES), 0)
    dst = lax.broadcasted_iota(jnp.int32, (MXU_DIM, LANES), 1)
    row0 = 0
    for (k0, k1, c0, c1) in BAND_TILES:
        row_block = (lax.broadcasted_iota(jnp.int32, (k1 - k0, MXU_DIM), 0) + k0) // LRU_BW
        for gate, w_ref in enumerate(w_refs):
            rows = w_ref[k0:k1, :]
            tile = jnp.zeros((k1 - k0, LANES), F32)
            for n in range(c0 // LRU_BW, min((c1 - 1) // LRU_BW, LRU_BLOCKS - 1) + 1):
                place = jnp.where(src == dst + (c0 - n * LRU_BW), 1.0, 0.0).astype(BF16)
                tile = tile + _dot(jnp.where(row_block == n, rows, 0.0).astype(BF16), place)
            wband_ref[row0:row0 + (k1 - k0), gate * LANES:(gate + 1) * LANES] = tile.astype(BF16)
        row0 += k1 - k0


def _mixer_kernel(x_ref, vec_ref, wqkvg_ref, wf_ref, wxl_ref, wlg_ref, wgab_ref, wf2_ref, gwo_ref,
                  wa_ref, wx_ref, lwo_ref, wout_ref,
                  out_ref,
                  st_ref, xl_ref, xprev_ref, hc_ref, o_ref, hl_ref, ac_ref, slab_ref, mb_ref, wband_ref):
    @pl.when(pl.program_id(0) == 0)
    def _():
        st_ref[...] = jnp.zeros_like(st_ref)
        xprev_ref[...] = jnp.zeros_like(xprev_ref)
        hc_ref[...] = jnp.zeros_like(hc_ref)
        _build_band((wa_ref, wx_ref), wband_ref)

    for sb in range(x_ref.shape[0] // SEQ_BLOCK):
        rows = slice(sb * SEQ_BLOCK, (sb + 1) * SEQ_BLOCK)
        _mixer_block(x_ref.at[rows, :], vec_ref, wqkvg_ref, wf_ref, wxl_ref, wlg_ref, wgab_ref, wf2_ref, gwo_ref,
                     lwo_ref, wout_ref, out_ref.at[rows, :],
                     st_ref, xl_ref, xprev_ref, hc_ref, o_ref, hl_ref, ac_ref, slab_ref, mb_ref, wband_ref)


def _mixer_block(x_ref, vec_ref, wqkvg_ref, wf_ref, wxl_ref, wlg_ref, wgab_ref, wf2_ref, gwo_ref,
                 lwo_ref, wout_ref, out_ref,
                 st_ref, xl_ref, xprev_ref, hc_ref, o_ref, hl_ref, ac_ref, slab_ref, mb_ref, wband_ref):
    T = x_ref.shape[0]
    n_chunks = T // GLA_CHUNK
    seg_len = T // SUBLANES

    def vec(r, n=LRU_PAD):
        return vec_ref[r:r + 1, :n]

    x = x_ref[...]
    hb = _rms_norm(x, vec(V_ATTN_NORM, D_MODEL)).astype(BF16)

    f_low = _dot(hb, wf_ref[...]).astype(BF16)
    z = _dot(f_low, wf2_ref[...]) + vec(V_GLA_BF2, GLA_KEY)
    hbp = _segment_major(hb)
    xl = _dot(hbp, wxl_ref[...])
    xl_ref[LRU_HALO:LRU_HALO + T, :] = xl
    _fill_halo(xl_ref, xprev_ref, xl, LRU_HALO)
    gk = (jnp.minimum(z, 0.0) - jnp.log1p(jnp.exp(-jnp.abs(z)))) * (1.0 / GLA_GATE_NORM)

    row = lax.broadcasted_iota(jnp.int32, (T, T), 0)
    col = lax.broadcasted_iota(jnp.int32, (T, T), 1)
    intra = (row >= col) & ((row // GLA_CHUNK) == (col // GLA_CHUNK))
    tri = jnp.where(intra, 1.0, 0.0).astype(BF16)
    g1 = gk.astype(BF16)
    r1 = gk - g1.astype(F32)
    g2 = r1.astype(BF16)
    g3 = (r1 - g2.astype(F32)).astype(BF16)
    b = _dot(tri, g1) + _dot(tri, g2) + _dot(tri, g3)
    q = _dot(hb, wqkvg_ref[:, O_Q:O_K]) * (GLA_DK ** -0.5)
    k = _dot(hb, wqkvg_ref[:, O_K:O_V])
    v = _dot(hb, wqkvg_ref[:, O_V:O_G]).astype(BF16)

    xc = vec(V_LRU_CONV_B) + xl * vec(V_LRU_CONV_W + LRU_CONV - 1)
    for d in range(1, LRU_CONV):
        rows = slice(LRU_HALO - d * SUBLANES, LRU_HALO - d * SUBLANES + T)
        xc = xc + xl_ref[rows, :] * vec(V_LRU_CONV_W + LRU_CONV - 1 - d)
    xcb = xc.astype(BF16)
    row0 = 0
    r_parts, i_parts = [], []
    for (k0, k1, c0, c1) in BAND_TILES:
        both = _dot(xcb[:, k0:k1], wband_ref[row0:row0 + (k1 - k0), :])
        r_parts.append(both[:, :LANES])
        i_parts.append(both[:, LANES:])
        row0 += k1 - k0

    b_last_rows = [b[(c + 1) * GLA_CHUNK - 1:(c + 1) * GLA_CHUNK, :] for c in range(n_chunks)]
    b_last = jnp.concatenate([jnp.broadcast_to(r, (GLA_CHUNK, GLA_KEY)) for r in b_last_rows], axis=0)
    decay = [jnp.exp(r) for r in b_last_rows]
    q_t = (q * jnp.exp(b)).astype(BF16)
    k_t = (k * jnp.exp(-b)).astype(BF16)
    k_s = (k * jnp.exp(b_last - b)).astype(BF16)

    def chunk_expand(m):
        cols = []
        for c in range(n_chunks):
            parts = []
            if c > 0:
                parts.append(jnp.zeros((c * GLA_CHUNK, GLA_DK), m.dtype))
            parts.append(m[c * GLA_CHUNK:(c + 1) * GLA_CHUNK, :])
            if c < n_chunks - 1:
                parts.append(jnp.zeros((T - (c + 1) * GLA_CHUNK, GLA_DK), m.dtype))
            cols.append(jnp.concatenate(parts, axis=0))
        return jnp.concatenate(cols, axis=1)

    for hh in range(GLA_HEADS):
        ks_ = slice(hh * GLA_DK, (hh + 1) * GLA_DK)
        vs_ = slice(hh * GLA_DV, (hh + 1) * GLA_DV)
        qh = q_t[:, ks_]
        vh = v[:, vs_]
        a = jnp.where(intra, _dot_nt(qh, k_t[:, ks_]), 0.0).astype(BF16)
        ds_all = _dot_tn(vh, chunk_expand(k_s[:, ks_]))
        st = st_ref[hh]
        starts = []
        for c in range(n_chunks):
            starts.append(st.astype(BF16))
            st = st * decay[c][:, ks_] + ds_all[:, c * GLA_DK:(c + 1) * GLA_DK]
        st_ref[hh] = st
        o_ref[:, vs_] = _dot(a, vh) + _dot_nt(chunk_expand(qh), jnp.concatenate(starts, axis=1))

    g = _dot(hb, wqkvg_ref[:, O_G:O_F])
    lg = _dot(hbp, wlg_ref[...])
    gb = _dot(hbp, wgab_ref[:, D_MODEL:])
    ga = _dot(hb, wgab_ref[:, :D_MODEL])

    r_gate = _sigmoid(jnp.concatenate(r_parts, axis=1) + vec(V_LRU_BA))
    i_gate = _sigmoid(jnp.concatenate(i_parts, axis=1) + vec(V_LRU_BX))
    log_a = (-LRU_C) * r_gate * _softplus(vec(V_LRU_A_PARAM))
    a_s = jnp.exp(log_a)
    u_s = jnp.sqrt(-jnp.tanh(log_a) * (1.0 + a_s * a_s)) * (i_gate * xc)

    h_loc = u_s[0:SUBLANES, :]
    a_cum = a_s[0:SUBLANES, :]
    hl_ref[0:SUBLANES, :] = h_loc
    ac_ref[0:SUBLANES, :] = a_cum
    for r in range(1, seg_len):
        rs = slice(r * SUBLANES, (r + 1) * SUBLANES)
        h_loc = a_s[rs, :] * h_loc + u_s[rs, :]
        a_cum = a_s[rs, :] * a_cum
        hl_ref[rs, :] = h_loc
        ac_ref[rs, :] = a_cum
    sub = lax.broadcasted_iota(jnp.int32, (SUBLANES, LRU_PAD), 0)
    e_s, p_s = h_loc, a_cum
    s = 1
    while s < SUBLANES:
        m = sub >= s
        e_s = jnp.where(m, e_s + p_s * pltpu.roll(e_s, s, 0), e_s)
        p_s = jnp.where(m, p_s * pltpu.roll(p_s, s, 0), p_s)
        s *= 2
    h_carry = hc_ref[...]
    seg_end = e_s + p_s * h_carry
    seg_start = jnp.where(sub == 0, h_carry, pltpu.roll(seg_end, 1, 0))
    hc_ref[...] = seg_end[SUBLANES - 1:SUBLANES, :]
    for r in range(seg_len):
        rs = slice(r * SUBLANES, (r + 1) * SUBLANES)
        hl_ref[rs, :] = hl_ref[rs, :] + ac_ref[rs, :] * seg_start

    og = []
    for hh in range(GLA_HEADS):
        vs_ = slice(hh * GLA_DV, (hh + 1) * GLA_DV)
        gh = g[:, vs_]
        og.append(_rms_norm(o_ref[:, vs_], vec(V_GLA_ONORM, GLA_DV)) * (gh * _sigmoid(gh)))
    y_a = _dot(jnp.concatenate(og, axis=1).astype(BF16), gwo_ref[...])

    y_b = _dot((hl_ref[...] * _gelu_tanh(lg)).astype(BF16), lwo_ref[...])
    gated_a = _sigmoid(ga) * y_a
    _to_token_major(_sigmoid(gb) * y_b, slab_ref, mb_ref)

    merged = (gated_a + mb_ref[...]).astype(BF16)
    out_ref[...] = x + _dot(merged, wout_ref[...])


def _ffn_kernel(x_ref, vec_ref, wup_ref, wdown_ref, out_ref,
                u_ref, uprev_ref, slab_ref, y_ref):
    T = x_ref.shape[0]

    @pl.when(pl.program_id(0) == 0)
    def _():
        uprev_ref[...] = jnp.zeros_like(uprev_ref)

    def vec(r, n=2 * D_FF):
        return vec_ref[r:r + 1, :n]

    n_sub = u_ref.shape[0]
    H = T // n_sub
    us = []
    for sb in range(n_sub):
        x = x_ref[sb * H:(sb + 1) * H, :]
        hbp = _segment_major(_rms_norm(x, vec(V_FFN_NORM, D_MODEL)).astype(BF16))
        u = _dot(hbp, wup_ref[...])
        u_ref[sb, FFN_HALO:FFN_HALO + H, :] = u
        _fill_halo(u_ref.at[sb], uprev_ref, u, FFN_HALO)
        us.append(u)
    for sb in range(n_sub):
        uc = vec(V_FFN_CONV_B) + us[sb] * vec(V_FFN_CONV_W + FFN_CONV - 1)
        for d in range(1, FFN_CONV):
            rows = slice(FFN_HALO - d * SUBLANES, FFN_HALO - d * SUBLANES + H)
            uc = uc + u_ref[sb, rows, :] * vec(V_FFN_CONV_W + FFN_CONV - 1 - d)
        act = (_gelu_tanh(uc[:, :D_FF]) * uc[:, D_FF:]).astype(BF16)
        _to_token_major(_dot(act, wdown_ref[...]), slab_ref.at[sb], y_ref, sb * H)
        rows = slice(sb * H, (sb + 1) * H)
        out_ref[rows, :] = _rms_norm(x_ref[rows, :] + y_ref[rows, :], vec(V_FINAL_NORM, D_MODEL))


def _const_spec(shape):
    nd = len(shape)
    return pl.BlockSpec(shape, lambda i, _nd=nd: (0,) * _nd, pipeline_mode=pl.Buffered(1))


def _pad_cols(w, n):
    return jnp.pad(w, ((0, 0), (0, n - w.shape[1])))


def _block_rows(w):
    rows = w.reshape(LRU_WIDTH, LRU_BW).astype(F32)
    return jnp.pad(rows, ((0, LRU_PAD - LRU_WIDTH), (0, MXU_DIM - LRU_BW)))


def _pack_rows(vectors, n):
    return jnp.concatenate([_pad_cols(v.reshape(-1, v.shape[-1]).astype(F32), n) for v in vectors], axis=0)


@jax.jit
def _forward(x, attn_norm, w_in, gla_wf2, gla_bf2, gla_onorm, gla_wo, lru_conv_w, lru_conv_b, lru_wa, lru_ba,
             lru_wx, lru_bx, lru_a_param, lru_wo, w_out, ffn_norm, ffn_wup, ffn_conv_w, ffn_conv_b, ffn_wdown,
             final_norm):
    B, S, D = x.shape
    assert B == 1 and D == D_MODEL and S % SEQ_BLOCK == 0 and attn_norm.shape[0] == 1
    T = SEQ_BLOCK
    x2 = x.reshape(S, D)

    wi = w_in[0]
    w_qkvg = wi[:, O_Q:O_F].astype(BF16)
    w_f = _pad_cols(wi[:, O_F:O_XL], RANK_PAD).astype(BF16)
    w_xl = wi[:, O_XL:O_XL + LRU_PAD].astype(BF16)
    w_lg = wi[:, O_LG:O_LG + LRU_PAD].astype(BF16)
    w_gab = wi[:, O_GA:O_END].astype(BF16)
    wf2 = jnp.pad(gla_wf2[0], ((0, RANK_PAD - GLA_GATE_RANK), (0, 0)))
    lwo = jnp.pad(lru_wo[0], ((0, LRU_PAD - LRU_WIDTH), (0, 0)))

    mixer_vec = _pack_rows([attn_norm[0], gla_bf2[0], gla_onorm[0], lru_conv_b[0], lru_ba[0], lru_bx[0],
                            lru_a_param[0], lru_conv_w[0]], LRU_PAD)
    mixer_in = [
        x2, mixer_vec, w_qkvg, w_f, w_xl, w_lg, w_gab, wf2.astype(BF16), gla_wo[0].astype(BF16),
        _block_rows(lru_wa[0]), _block_rows(lru_wx[0]), lwo.astype(BF16), w_out[0].astype(BF16),
    ]
    assert S % (MIX_SUB * T) == 0
    blk = pl.BlockSpec((MIX_SUB * T, D), lambda i: (i, 0))
    params = pltpu.CompilerParams(dimension_semantics=("arbitrary",), vmem_limit_bytes=VMEM_LIMIT_BYTES)
    x1 = pl.pallas_call(
        _mixer_kernel,
        grid=(S // (MIX_SUB * T),),
        in_specs=[blk] + [_const_spec(a.shape) for a in mixer_in[1:]],
        out_specs=blk,
        out_shape=jax.ShapeDtypeStruct((S, D), F32),
        scratch_shapes=[
            pltpu.VMEM((GLA_HEADS, GLA_DV, GLA_DK), F32),
            pltpu.VMEM((LRU_HALO + T, LRU_PAD), F32),
            pltpu.VMEM((LRU_HALO, LRU_PAD), F32),
            pltpu.VMEM((1, LRU_PAD), F32),
            pltpu.VMEM((T, GLA_VAL), F32),
            pltpu.VMEM((T, LRU_PAD), F32),
            pltpu.VMEM((T, LRU_PAD), F32),
            pltpu.VMEM((D // LANES, T, LANES), F32),
            pltpu.VMEM((T, D), F32),
            pltpu.VMEM((BAND_ROWS, 2 * LANES), BF16),
        ],
        compiler_params=params,
        name="mixer",
    )(*mixer_in)

    ffn_vec = _pack_rows([ffn_norm[0], final_norm, ffn_conv_b[0], ffn_conv_w[0]], 2 * D_FF)
    ffn_in = [x1, ffn_vec, ffn_wup[0].astype(BF16), ffn_wdown[0].astype(BF16)]
    assert S % (FFN_SUB * T) == 0
    ffn_blk = pl.BlockSpec((FFN_SUB * T, D), lambda i: (i, 0))
    out = pl.pallas_call(
        _ffn_kernel,
        grid=(S // (FFN_SUB * T),),
        in_specs=[ffn_blk] + [_const_spec(a.shape) for a in ffn_in[1:]],
        out_specs=ffn_blk,
        out_shape=jax.ShapeDtypeStruct((S, D), F32),
        scratch_shapes=[
            pltpu.VMEM((FFN_SUB, FFN_HALO + T, 2 * D_FF), F32),
            pltpu.VMEM((FFN_HALO, 2 * D_FF), F32),
            pltpu.VMEM((FFN_SUB, D // LANES, T, LANES), F32),
            pltpu.VMEM((FFN_SUB * T, D), F32),
        ],
        compiler_params=params,
        name="ffn",
    )(*ffn_in)
    return out.reshape(B, S, D)


def kernel(x, attn_norm, w_in, gla_wf2, gla_bf2, gla_onorm, gla_wo, lru_conv_w, lru_conv_b, lru_wa, lru_ba, lru_wx, lru_bx, lru_a_param, lru_wo, w_out, ffn_norm, ffn_wup, ffn_conv_w, ffn_conv_b, ffn_wdown, final_norm):
    return _forward(x, attn_norm, w_in, gla_wf2, gla_bf2, gla_onorm, gla_wo, lru_conv_w, lru_conv_b, lru_wa, lru_ba,
                    lru_wx, lru_bx, lru_a_param, lru_wo, w_out, ffn_norm, ffn_wup, ffn_conv_w, ffn_conv_b,
                    ffn_wdown, final_norm)
```

```python
import jax
import jax.numpy as jnp
import numpy as np
from jax import lax
from jax.experimental import pallas as pl
from jax.experimental.pallas import tpu as pltpu

D_MODEL = 1024
GLA_HEADS = 4
GLA_DK = 128
GLA_DV = 256
GLA_KEY = GLA_HEADS * GLA_DK
GLA_VAL = GLA_HEADS * GLA_DV
GLA_GATE_RANK = 16
GLA_GATE_NORM = 16.0
GLA_CHUNK = 64
LRU_WIDTH = 1344
LRU_BLOCKS = 8
LRU_BW = LRU_WIDTH // LRU_BLOCKS
LRU_CONV = 4
LRU_C = 8.0
D_FF = 3 * D_MODEL
FFN_CONV = 3
EPS = 1e-6

LANES = 128
SUBLANES = 8
MXU_DIM = 256
VMEM_LIMIT_BYTES = 58 * 1024 * 1024

LRU_PAD = ((LRU_WIDTH + LANES - 1) // LANES) * LANES
RANK_PAD = LANES
SEQ_BLOCK = 256
LRU_HALO = (LRU_CONV - 1) * SUBLANES
FFN_HALO = (FFN_CONV - 1) * SUBLANES
FFN_SUB = 2
MIX_SUB = 1

(O_Q, O_K, O_V, O_G, O_F, O_XL, O_LG, O_GA, O_GB, O_END) = [int(v) for v in np.cumsum(
    [0, GLA_KEY, GLA_KEY, GLA_VAL, GLA_VAL, GLA_GATE_RANK, LRU_WIDTH, LRU_WIDTH, D_MODEL, D_MODEL])]

WINDOW_STARTS = (O_Q, O_F, O_XL, O_LG, O_GA)
WINDOW_WIDTHS = (O_F - O_Q, RANK_PAD, LRU_PAD, LRU_PAD, O_END - O_GA)
PACK_ROWS = 128

(V_ATTN_NORM, V_GLA_BF2, V_GLA_ONORM, V_LRU_CONV_B, V_LRU_BA, V_LRU_BX, V_LRU_A_PARAM, V_LRU_CONV_W) = range(8)
(V_FFN_NORM, V_FINAL_NORM, V_FFN_CONV_B, V_FFN_CONV_W) = range(4)

BF16 = jnp.bfloat16
F32 = jnp.float32


def _band_tiles():
    tiles = []
    for c0 in range(0, LRU_PAD, LANES):
        c1 = min(c0 + LANES, LRU_PAD)
        last_col = min(c1, LRU_WIDTH) - 1
        b_lo = c0 // LRU_BW
        b_hi = last_col // LRU_BW
        k0 = (b_lo * LRU_BW) // LANES * LANES
        k1 = min(-(-((b_hi + 1) * LRU_BW) // LANES) * LANES, LRU_PAD)
        tiles.append((k0, k1, c0, c1))
    return tuple(tiles)


BAND_TILES = _band_tiles()
BAND_ROWS = sum(k1 - k0 for k0, k1, _, _ in BAND_TILES)


def _dot(a, b):
    return jnp.dot(a, b, preferred_element_type=F32)


def _dot_nt(a, b):
    return lax.dot_general(a, b, (((1,), (1,)), ((), ())), preferred_element_type=F32)


def _dot_tn(a, b):
    return lax.dot_general(a, b, (((0,), (0,)), ((), ())), preferred_element_type=F32)


def _sigmoid(x):
    return 1.0 / (1.0 + jnp.exp(-x))


def _softplus(x):
    return jnp.maximum(x, 0.0) + jnp.log1p(jnp.exp(-jnp.abs(x)))


def _gelu_tanh(x):
    c = float(np.sqrt(2.0 / np.pi))
    hx = 0.5 * x
    return hx + hx * jnp.tanh(x * (c + (0.044715 * c) * (x * x)))


def _rms_norm(x, g):
    ms = jnp.mean(x * x, axis=-1, keepdims=True)
    return x * lax.rsqrt(ms + EPS) * g


def _segment_major(hb):
    T = hb.shape[0]
    seg_len = T // SUBLANES
    p = lax.broadcasted_iota(jnp.int32, (T, T), 0)
    t = lax.broadcasted_iota(jnp.int32, (T, T), 1)
    perm = jnp.where(t == (p % SUBLANES) * seg_len + p // SUBLANES, 1.0, 0.0).astype(BF16)
    return _dot(perm, hb).astype(BF16)


def _fill_halo(buf_ref, prev_ref, cur, halo, cols=slice(None)):
    T = cur.shape[0]
    sub = lax.broadcasted_iota(jnp.int32, (SUBLANES, cur.shape[1]), 0)
    for off in range(0, halo, SUBLANES):
        rows = slice(off, off + SUBLANES)
        tail = cur[T - halo + off:T - halo + off + SUBLANES, :]
        buf_ref[rows, cols] = pltpu.roll(jnp.where(sub == SUBLANES - 1, prev_ref[rows, cols], tail), 1, 0)
    prev_ref[:, cols] = cur[T - halo:T, :]


def _to_token_major(val, slab_ref, dst_ref, row0=0):
    T, d = val.shape
    seg_len = T // SUBLANES
    for n in range(d // LANES):
        slab_ref[n] = val[:, n * LANES:(n + 1) * LANES]
    for j in range(T // SUBLANES):
        t0 = j * SUBLANES
        start = (t0 % seg_len) * SUBLANES + t0 // seg_len
        for n in range(d // LANES):
            dst_ref[row0 + t0:row0 + t0 + SUBLANES, n * LANES:(n + 1) * LANES] = (
                slab_ref[n, pl.ds(start, SUBLANES, stride=SUBLANES), :])


def _build_band(w_refs, wband_ref):
    src = lax.broadcasted_iota(jnp.int32, (MXU_DIM, LANES), 0)
    dst = lax.broadcasted_iota(jnp.int32, (MXU_DIM, LANES), 1)
    row0 = 0
    for (k0, k1, c0, c1) in BAND_TILES:
        row_block = (lax.broadcasted_iota(jnp.int32, (k1 - k0, MXU_DIM), 0) + k0) // LRU_BW
        for gate, w_ref in enumerate(w_refs):
            rows = w_ref[k0:k1, :]
            tile = jnp.zeros((k1 - k0, LANES), F32)
            for n in range(c0 // LRU_BW, min((c1 - 1) // LRU_BW, LRU_BLOCKS - 1) + 1):
                place = jnp.where(src == dst + (c0 - n * LRU_BW), 1.0, 0.0).astype(BF16)
                tile = tile + _dot(jnp.where(row_block == n, rows, 0.0).astype(BF16), place)
            wband_ref[row0:row0 + (k1 - k0), gate * LANES:(gate + 1) * LANES] = tile.astype(BF16)
        row0 += k1 - k0


def _mixer_kernel(x_ref, vec_ref, wqkvg_ref, wf_ref, wxl_ref, wlg_ref, wgab_ref, wf2_ref, gwo_ref,
                  wa_ref, wx_ref, lwo_ref, wout_ref,
                  out_ref,
                  st_ref, xl_ref, xprev_ref, hc_ref, o_ref, hl_ref, ac_ref, slab_ref, mb_ref, wband_ref):
    @pl.when(pl.program_id(0) == 0)
    def _():
        st_ref[...] = jnp.zeros_like(st_ref)
        xprev_ref[...] = jnp.zeros_like(xprev_ref)
        hc_ref[...] = jnp.zeros_like(hc_ref)
        _build_band((wa_ref, wx_ref), wband_ref)

    for sb in range(x_ref.shape[0] // SEQ_BLOCK):
        rows = slice(sb * SEQ_BLOCK, (sb + 1) * SEQ_BLOCK)
        _mixer_block(x_ref.at[rows, :], vec_ref, wqkvg_ref, wf_ref, wxl_ref, wlg_ref, wgab_ref, wf2_ref, gwo_ref,
                     lwo_ref, wout_ref, out_ref.at[rows, :],
                     st_ref, xl_ref, xprev_ref, hc_ref, o_ref, hl_ref, ac_ref, slab_ref, mb_ref, wband_ref)


def _mixer_block(x_ref, vec_ref, wqkvg_ref, wf_ref, wxl_ref, wlg_ref, wgab_ref, wf2_ref, gwo_ref,
                 lwo_ref, wout_ref, out_ref,
                 st_ref, xl_ref, xprev_ref, hc_ref, o_ref, hl_ref, ac_ref, slab_ref, mb_ref, wband_ref):
    T = x_ref.shape[0]
    n_chunks = T // GLA_CHUNK
    seg_len = T // SUBLANES

    def vec(r, n=LRU_PAD):
        return vec_ref[r:r + 1, :n]

    x = x_ref[...]
    hb = _rms_norm(x, vec(V_ATTN_NORM, D_MODEL)).astype(BF16)

    f_low = _dot(hb, wf_ref[...]).astype(BF16)
    z = _dot(f_low, wf2_ref[...]) + vec(V_GLA_BF2, GLA_KEY)
    hbp = _segment_major(hb)
    xl = _dot(hbp, wxl_ref[...])
    xl_ref[LRU_HALO:LRU_HALO + T, :] = xl
    _fill_halo(xl_ref, xprev_ref, xl, LRU_HALO)
    gk = (jnp.minimum(z, 0.0) - jnp.log1p(jnp.exp(-jnp.abs(z)))) * (1.0 / GLA_GATE_NORM)

    row = lax.broadcasted_iota(jnp.int32, (T, T), 0)
    col = lax.broadcasted_iota(jnp.int32, (T, T), 1)
    intra = (row >= col) & ((row // GLA_CHUNK) == (col // GLA_CHUNK))
    tri = jnp.where(intra, 1.0, 0.0).astype(BF16)
    g1 = gk.astype(BF16)
    r1 = gk - g1.astype(F32)
    g2 = r1.astype(BF16)
    g3 = (r1 - g2.astype(F32)).astype(BF16)
    b = _dot(tri, g1) + _dot(tri, g2) + _dot(tri, g3)
    q = _dot(hb, wqkvg_ref[:, O_Q:O_K]) * (GLA_DK ** -0.5)
    k = _dot(hb, wqkvg_ref[:, O_K:O_V])
    v = _dot(hb, wqkvg_ref[:, O_V:O_G]).astype(BF16)

    xc = vec(V_LRU_CONV_B) + xl * vec(V_LRU_CONV_W + LRU_CONV - 1)
    for d in range(1, LRU_CONV):
        rows = slice(LRU_HALO - d * SUBLANES, LRU_HALO - d * SUBLANES + T)
        xc = xc + xl_ref[rows, :] * vec(V_LRU_CONV_W + LRU_CONV - 1 - d)
    xcb = xc.astype(BF16)
    row0 = 0
    r_parts, i_parts = [], []
    for (k0, k1, c0, c1) in BAND_TILES:
        both = _dot(xcb[:, k0:k1], wband_ref[row0:row0 + (k1 - k0), :])
        r_parts.append(both[:, :LANES])
        i_parts.append(both[:, LANES:])
        row0 += k1 - k0

    b_last_rows = [b[(c + 1) * GLA_CHUNK - 1:(c + 1) * GLA_CHUNK, :] for c in range(n_chunks)]
    b_last = jnp.concatenate([jnp.broadcast_to(r, (GLA_CHUNK, GLA_KEY)) for r in b_last_rows], axis=0)
    decay = [jnp.exp(r) for r in b_last_rows]
    q_t = (q * jnp.exp(b)).astype(BF16)
    k_t = (k * jnp.exp(-b)).astype(BF16)
    k_s = (k * jnp.exp(b_last - b)).astype(BF16)

    def chunk_expand(m):
        cols = []
        for c in range(n_chunks):
            parts = []
            if c > 0:
                parts.append(jnp.zeros((c * GLA_CHUNK, GLA_DK), m.dtype))
            parts.append(m[c * GLA_CHUNK:(c + 1) * GLA_CHUNK, :])
            if c < n_chunks - 1:
                parts.append(jnp.zeros((T - (c + 1) * GLA_CHUNK, GLA_DK), m.dtype))
            cols.append(jnp.concatenate(parts, axis=0))
        return jnp.concatenate(cols, axis=1)

    for hh in range(GLA_HEADS):
        ks_ = slice(hh * GLA_DK, (hh + 1) * GLA_DK)
        vs_ = slice(hh * GLA_DV, (hh + 1) * GLA_DV)
        qh = q_t[:, ks_]
        vh = v[:, vs_]
        a = jnp.where(intra, _dot_nt(qh, k_t[:, ks_]), 0.0).astype(BF16)
        ds_all = _dot_tn(vh, chunk_expand(k_s[:, ks_]))
        st = st_ref[hh]
        starts = []
        for c in range(n_chunks):
            starts.append(st.astype(BF16))
            st = st * decay[c][:, ks_] + ds_all[:, c * GLA_DK:(c + 1) * GLA_DK]
        st_ref[hh] = st
        o_ref[:, vs_] = _dot(a, vh) + _dot_nt(chunk_expand(qh), jnp.concatenate(starts, axis=1))

    g = _dot(hb, wqkvg_ref[:, O_G:O_F])
    lg = _dot(hbp, wlg_ref[...])
    gb = _dot(hbp, wgab_ref[:, D_MODEL:])
    ga = _dot(hb, wgab_ref[:, :D_MODEL])

    r_gate = _sigmoid(jnp.concatenate(r_parts, axis=1) + vec(V_LRU_BA))
    i_gate = _sigmoid(jnp.concatenate(i_parts, axis=1) + vec(V_LRU_BX))
    log_a = (-LRU_C) * r_gate * _softplus(vec(V_LRU_A_PARAM))
    a_s = jnp.exp(log_a)
    u_s = jnp.sqrt(-jnp.tanh(log_a) * (1.0 + a_s * a_s)) * (i_gate * xc)

    h_loc = u_s[0:SUBLANES, :]
    a_cum = a_s[0:SUBLANES, :]
    hl_ref[0:SUBLANES, :] = h_loc
    ac_ref[0:SUBLANES, :] = a_cum
    for r in range(1, seg_len):
        rs = slice(r * SUBLANES, (r + 1) * SUBLANES)
        h_loc = a_s[rs, :] * h_loc + u_s[rs, :]
        a_cum = a_s[rs, :] * a_cum
        hl_ref[rs, :] = h_loc
        ac_ref[rs, :] = a_cum
    sub = lax.broadcasted_iota(jnp.int32, (SUBLANES, LRU_PAD), 0)
    e_s, p_s = h_loc, a_cum
    s = 1
    while s < SUBLANES:
        m = sub >= s
        e_s = jnp.where(m, e_s + p_s * pltpu.roll(e_s, s, 0), e_s)
        p_s = jnp.where(m, p_s * pltpu.roll(p_s, s, 0), p_s)
        s *= 2
    h_carry = hc_ref[...]
    seg_end = e_s + p_s * h_carry
    seg_start = jnp.where(sub == 0, h_carry, pltpu.roll(seg_end, 1, 0))
    hc_ref[...] = seg_end[SUBLANES - 1:SUBLANES, :]
    for r in range(seg_len):
        rs = slice(r * SUBLANES, (r + 1) * SUBLANES)
        hl_ref[rs, :] = hl_ref[rs, :] + ac_ref[rs, :] * seg_start

    og = []
    for hh in range(GLA_HEADS):
        vs_ = slice(hh * GLA_DV, (hh + 1) * GLA_DV)
        gh = g[:, vs_]
        og.append(_rms_norm(o_ref[:, vs_], vec(V_GLA_ONORM, GLA_DV)) * (gh * _sigmoid(gh)))
    y_a = _dot(jnp.concatenate(og, axis=1).astype(BF16), gwo_ref[...])

    y_b = _dot((hl_ref[...] * _gelu_tanh(lg)).astype(BF16), lwo_ref[...])
    gated_a = _sigmoid(ga) * y_a
    _to_token_major(_sigmoid(gb) * y_b, slab_ref, mb_ref)

    merged = (gated_a + mb_ref[...]).astype(BF16)
    out_ref[...] = x + _dot(merged, wout_ref[...])


def _ffn_kernel(x_ref, vec_ref, wup_ref, wdown_ref, out_ref,
                u_ref, uprev_ref, slab_ref, y_ref):
    T = x_ref.shape[0]

    @pl.when(pl.program_id(0) == 0)
    def _():
        uprev_ref[...] = jnp.zeros_like(uprev_ref)

    def vec(r, n=2 * D_FF):
        return vec_ref[r:r + 1, :n]

    n_sub = u_ref.shape[0]
    H = T // n_sub
    us = []
    for sb in range(n_sub):
        x = x_ref[sb * H:(sb + 1) * H, :]
        hbp = _segment_major(_rms_norm(x, vec(V_FFN_NORM, D_MODEL)).astype(BF16))
        u = _dot(hbp, wup_ref[...])
        u_ref[sb, FFN_HALO:FFN_HALO + H, :] = u
        _fill_halo(u_ref.at[sb], uprev_ref, u, FFN_HALO)
        us.append(u)
    for sb in range(n_sub):
        uc = vec(V_FFN_CONV_B) + us[sb] * vec(V_FFN_CONV_W + FFN_CONV - 1)
        for d in range(1, FFN_CONV):
            rows = slice(FFN_HALO - d * SUBLANES, FFN_HALO - d * SUBLANES + H)
            uc = uc + u_ref[sb, rows, :] * vec(V_FFN_CONV_W + FFN_CONV - 1 - d)
        act = (_gelu_tanh(uc[:, :D_FF]) * uc[:, D_FF:]).astype(BF16)
        _to_token_major(_dot(act, wdown_ref[...]), slab_ref.at[sb], y_ref, sb * H)
        rows = slice(sb * H, (sb + 1) * H)
        out_ref[rows, :] = _rms_norm(x_ref[rows, :] + y_ref[rows, :], vec(V_FINAL_NORM, D_MODEL))


def _window_kernel(w_ref, qkvg_ref, f_ref, xl_ref, lg_ref, gab_ref):
    w = w_ref[...]
    for out_ref, start in zip((qkvg_ref, f_ref, xl_ref, lg_ref, gab_ref), WINDOW_STARTS):
        out_ref[...] = w[:, start:start + out_ref.shape[1]].astype(BF16)


def _const_spec(shape):
    nd = len(shape)
    return pl.BlockSpec(shape, lambda i, _nd=nd: (0,) * _nd, pipeline_mode=pl.Buffered(1))


def _pad_cols(w, n):
    return jnp.pad(w, ((0, 0), (0, n - w.shape[1])))


def _block_rows(w):
    rows = w.reshape(LRU_WIDTH, LRU_BW).astype(F32)
    return jnp.pad(rows, ((0, LRU_PAD - LRU_WIDTH), (0, MXU_DIM - LRU_BW)))


def _pack_rows(vectors, n):
    return jnp.concatenate([_pad_cols(v.reshape(-1, v.shape[-1]).astype(F32), n) for v in vectors], axis=0)


@jax.jit
def _forward(x, attn_norm, w_in, gla_wf2, gla_bf2, gla_onorm, gla_wo, lru_conv_w, lru_conv_b, lru_wa, lru_ba,
             lru_wx, lru_bx, lru_a_param, lru_wo, w_out, ffn_norm, ffn_wup, ffn_conv_w, ffn_conv_b, ffn_wdown,
             final_norm):
    B, S, D = x.shape
    assert B == 1 and D == D_MODEL and S % SEQ_BLOCK == 0 and attn_norm.shape[0] == 1
    T = SEQ_BLOCK
    x2 = x.reshape(S, D)

    w_qkvg, w_f, w_xl, w_lg, w_gab = pl.pallas_call(
        _window_kernel,
        grid=(D // PACK_ROWS,),
        in_specs=[pl.BlockSpec((PACK_ROWS, O_END), lambda i: (i, 0))],
        out_specs=[pl.BlockSpec((PACK_ROWS, n), lambda i: (i, 0)) for n in WINDOW_WIDTHS],
        out_shape=[jax.ShapeDtypeStruct((D, n), BF16) for n in WINDOW_WIDTHS],
        compiler_params=pltpu.CompilerParams(dimension_semantics=("arbitrary",)),
        name="w_in_windows",
    )(w_in[0])
    wf2 = jnp.pad(gla_wf2[0], ((0, RANK_PAD - GLA_GATE_RANK), (0, 0)))
    lwo = jnp.pad(lru_wo[0], ((0, LRU_PAD - LRU_WIDTH), (0, 0)))

    mixer_vec = _pack_rows([attn_norm[0], gla_bf2[0], gla_onorm[0], lru_conv_b[0], lru_ba[0], lru_bx[0],
                            lru_a_param[0], lru_conv_w[0]], LRU_PAD)
    mixer_in = [
        x2, mixer_vec, w_qkvg, w_f, w_xl, w_lg, w_gab, wf2.astype(BF16), gla_wo[0].astype(BF16),
        _block_rows(lru_wa[0]), _block_rows(lru_wx[0]), lwo.astype(BF16), w_out[0].astype(BF16),
    ]
    assert S % (MIX_SUB * T) == 0
    blk = pl.BlockSpec((MIX_SUB * T, D), lambda i: (i, 0))
    params = pltpu.CompilerParams(dimension_semantics=("arbitrary",), vmem_limit_bytes=VMEM_LIMIT_BYTES)
    x1 = pl.pallas_call(
        _mixer_kernel,
        grid=(S // (MIX_SUB * T),),
        in_specs=[blk] + [_const_spec(a.shape) for a in mixer_in[1:]],
        out_specs=blk,
        out_shape=jax.ShapeDtypeStruct((S, D), F32),
        scratch_shapes=[
            pltpu.VMEM((GLA_HEADS, GLA_DV, GLA_DK), F32),
            pltpu.VMEM((LRU_HALO + T, LRU_PAD), F32),
            pltpu.VMEM((LRU_HALO, LRU_PAD), F32),
            pltpu.VMEM((1, LRU_PAD), F32),
            pltpu.VMEM((T, GLA_VAL), F32),
            pltpu.VMEM((T, LRU_PAD), F32),
            pltpu.VMEM((T, LRU_PAD), F32),
            pltpu.VMEM((D // LANES, T, LANES), F32),
            pltpu.VMEM((T, D), F32),
            pltpu.VMEM((BAND_ROWS, 2 * LANES), BF16),
        ],
        compiler_params=params,
        name="mixer",
    )(*mixer_in)

    ffn_vec = _pack_rows([ffn_norm[0], final_norm, ffn_conv_b[0], ffn_conv_w[0]], 2 * D_FF)
    ffn_in = [x1, ffn_vec, ffn_wup[0].astype(BF16), ffn_wdown[0].astype(BF16)]
    assert S % (FFN_SUB * T) == 0
    ffn_blk = pl.BlockSpec((FFN_SUB * T, D), lambda i: (i, 0))
    out = pl.pallas_call(
        _ffn_kernel,
        grid=(S // (FFN_SUB * T),),
        in_specs=[ffn_blk] + [_const_spec(a.shape) for a in ffn_in[1:]],
        out_specs=ffn_blk,
        out_shape=jax.ShapeDtypeStruct((S, D), F32),
        scratch_shapes=[
            pltpu.VMEM((FFN_SUB, FFN_HALO + T, 2 * D_FF), F32),
            pltpu.VMEM((FFN_HALO, 2 * D_FF), F32),
            pltpu.VMEM((FFN_SUB, D // LANES, T, LANES), F32),
            pltpu.VMEM((FFN_SUB * T, D), F32),
        ],
        compiler_params=params,
        name="ffn",
    )(*ffn_in)
    return out.reshape(B, S, D)


def kernel(x, attn_norm, w_in, gla_wf2, gla_bf2, gla_onorm, gla_wo, lru_conv_w, lru_conv_b, lru_wa, lru_ba, lru_wx, lru_bx, lru_a_param, lru_wo, w_out, ffn_norm, ffn_wup, ffn_conv_w, ffn_conv_b, ffn_wdown, final_norm):
    return _forward(x, attn_norm, w_in, gla_wf2, gla_bf2, gla_onorm, gla_wo, lru_conv_w, lru_conv_b, lru_wa, lru_ba,
                    lru_wx, lru_bx, lru_a_param, lru_wo, w_out, ffn_norm, ffn_wup, ffn_conv_w, ffn_conv_b,
                    ffn_wdown, final_norm)
```

```python
import jax
import jax.numpy as jnp
import numpy as np
from jax import lax
from jax.experimental import pallas as pl
from jax.experimental.pallas import tpu as pltpu

D_MODEL = 1024
GLA_HEADS = 4
GLA_DK = 128
GLA_DV = 256
GLA_KEY = GLA_HEADS * GLA_DK
GLA_VAL = GLA_HEADS * GLA_DV
GLA_GATE_RANK = 16
GLA_GATE_NORM = 16.0
GLA_CHUNK = 64
LRU_WIDTH = 1344
LRU_BLOCKS = 8
LRU_BW = LRU_WIDTH // LRU_BLOCKS
LRU_CONV = 4
LRU_C = 8.0
D_FF = 3 * D_MODEL
FFN_CONV = 3
EPS = 1e-6

LANES = 128
SUBLANES = 8
MXU_DIM = 256
VMEM_LIMIT_BYTES = 58 * 1024 * 1024

LRU_PAD = ((LRU_WIDTH + LANES - 1) // LANES) * LANES
RANK_PAD = LANES
SEQ_BLOCK = 256
LRU_HALO = (LRU_CONV - 1) * SUBLANES
FFN_HALO = (FFN_CONV - 1) * SUBLANES
FFN_SUB = 2
MIX_SUB = 2

(O_Q, O_K, O_V, O_G, O_F, O_XL, O_LG, O_GA, O_GB, O_END) = [int(v) for v in np.cumsum(
    [0, GLA_KEY, GLA_KEY, GLA_VAL, GLA_VAL, GLA_GATE_RANK, LRU_WIDTH, LRU_WIDTH, D_MODEL, D_MODEL])]

(V_ATTN_NORM, V_GLA_BF2, V_GLA_ONORM, V_LRU_CONV_B, V_LRU_BA, V_LRU_BX, V_LRU_A_PARAM, V_LRU_CONV_W) = range(8)
(V_FFN_NORM, V_FINAL_NORM, V_FFN_CONV_B, V_FFN_CONV_W) = range(4)

BF16 = jnp.bfloat16
F32 = jnp.float32


def _band_tiles():
    tiles = []
    for c0 in range(0, LRU_PAD, LANES):
        c1 = min(c0 + LANES, LRU_PAD)
        last_col = min(c1, LRU_WIDTH) - 1
        b_lo = c0 // LRU_BW
        b_hi = last_col // LRU_BW
        k0 = (b_lo * LRU_BW) // LANES * LANES
        k1 = min(-(-((b_hi + 1) * LRU_BW) // LANES) * LANES, LRU_PAD)
        tiles.append((k0, k1, c0, c1))
    return tuple(tiles)


BAND_TILES = _band_tiles()
BAND_ROWS = sum(k1 - k0 for k0, k1, _, _ in BAND_TILES)


def _dot(a, b):
    return jnp.dot(a, b, preferred_element_type=F32)


def _dot_nt(a, b):
    return lax.dot_general(a, b, (((1,), (1,)), ((), ())), preferred_element_type=F32)


def _dot_tn(a, b):
    return lax.dot_general(a, b, (((0,), (0,)), ((), ())), preferred_element_type=F32)


def _sigmoid(x):
    return 1.0 / (1.0 + jnp.exp(-x))


def _softplus(x):
    return jnp.maximum(x, 0.0) + jnp.log1p(jnp.exp(-jnp.abs(x)))


def _gelu_tanh(x):
    c = float(np.sqrt(2.0 / np.pi))
    hx = 0.5 * x
    return hx + hx * jnp.tanh(x * (c + (0.044715 * c) * (x * x)))


def _rms_norm(x, g):
    ms = jnp.mean(x * x, axis=-1, keepdims=True)
    return x * lax.rsqrt(ms + EPS) * g


def _segment_major(hb):
    T = hb.shape[0]
    seg_len = T // SUBLANES
    p = lax.broadcasted_iota(jnp.int32, (T, T), 0)
    t = lax.broadcasted_iota(jnp.int32, (T, T), 1)
    perm = jnp.where(t == (p % SUBLANES) * seg_len + p // SUBLANES, 1.0, 0.0).astype(BF16)
    return _dot(perm, hb).astype(BF16)


def _fill_halo(buf_ref, prev_ref, cur, halo, cols=slice(None)):
    T = cur.shape[0]
    sub = lax.broadcasted_iota(jnp.int32, (SUBLANES, cur.shape[1]), 0)
    for off in range(0, halo, SUBLANES):
        rows = slice(off, off + SUBLANES)
        tail = cur[T - halo + off:T - halo + off + SUBLANES, :]
        buf_ref[rows, cols] = pltpu.roll(jnp.where(sub == SUBLANES - 1, prev_ref[rows, cols], tail), 1, 0)
    prev_ref[:, cols] = cur[T - halo:T, :]


def _to_token_major(val, slab_ref, dst_ref, row0=0):
    T, d = val.shape
    seg_len = T // SUBLANES
    for n in range(d // LANES):
        slab_ref[n] = val[:, n * LANES:(n + 1) * LANES]
    for j in range(T // SUBLANES):
        t0 = j * SUBLANES
        start = (t0 % seg_len) * SUBLANES + t0 // seg_len
        for n in range(d // LANES):
            dst_ref[row0 + t0:row0 + t0 + SUBLANES, n * LANES:(n + 1) * LANES] = (
                slab_ref[n, pl.ds(start, SUBLANES, stride=SUBLANES), :])


def _build_band(w_refs, wband_ref):
    src = lax.broadcasted_iota(jnp.int32, (MXU_DIM, LANES), 0)
    dst = lax.broadcasted_iota(jnp.int32, (MXU_DIM, LANES), 1)
    row0 = 0
    for (k0, k1, c0, c1) in BAND_TILES:
        row_block = (lax.broadcasted_iota(jnp.int32, (k1 - k0, MXU_DIM), 0) + k0) // LRU_BW
        for gate, w_ref in enumerate(w_refs):
            rows = w_ref[k0:k1, :]
            tile = jnp.zeros((k1 - k0, LANES), F32)
            for n in range(c0 // LRU_BW, min((c1 - 1) // LRU_BW, LRU_BLOCKS - 1) + 1):
                place = jnp.where(src == dst + (c0 - n * LRU_BW), 1.0, 0.0).astype(BF16)
                tile = tile + _dot(jnp.where(row_block == n, rows, 0.0).astype(BF16), place)
            wband_ref[row0:row0 + (k1 - k0), gate * LANES:(gate + 1) * LANES] = tile.astype(BF16)
        row0 += k1 - k0


def _mixer_kernel(x_ref, vec_ref, wqkvg_ref, wf_ref, wxl_ref, wlg_ref, wgab_ref, wf2_ref, gwo_ref,
                  wa_ref, wx_ref, lwo_ref, wout_ref,
                  out_ref,
                  st_ref, xl_ref, xprev_ref, hc_ref, o_ref, hl_ref, ac_ref, slab_ref, mb_ref, wband_ref):
    @pl.when(pl.program_id(0) == 0)
    def _():
        st_ref[...] = jnp.zeros_like(st_ref)
        xprev_ref[...] = jnp.zeros_like(xprev_ref)
        hc_ref[...] = jnp.zeros_like(hc_ref)
        _build_band((wa_ref, wx_ref), wband_ref)

    for sb in range(x_ref.shape[0] // SEQ_BLOCK):
        rows = slice(sb * SEQ_BLOCK, (sb + 1) * SEQ_BLOCK)
        _mixer_block(x_ref.at[rows, :], vec_ref, wqkvg_ref, wf_ref, wxl_ref, wlg_ref, wgab_ref, wf2_ref, gwo_ref,
                     lwo_ref, wout_ref, out_ref.at[rows, :],
                     st_ref, xl_ref, xprev_ref, hc_ref, o_ref, hl_ref, ac_ref, slab_ref, mb_ref, wband_ref)


def _mixer_block(x_ref, vec_ref, wqkvg_ref, wf_ref, wxl_ref, wlg_ref, wgab_ref, wf2_ref, gwo_ref,
                 lwo_ref, wout_ref, out_ref,
                 st_ref, xl_ref, xprev_ref, hc_ref, o_ref, hl_ref, ac_ref, slab_ref, mb_ref, wband_ref):
    T = x_ref.shape[0]
    n_chunks = T // GLA_CHUNK
    seg_len = T // SUBLANES

    def vec(r, n=LRU_PAD):
        return vec_ref[r:r + 1, :n]

    x = x_ref[...]
    hb = _rms_norm(x, vec(V_ATTN_NORM, D_MODEL)).astype(BF16)

    f_low = _dot(hb, wf_ref[...]).astype(BF16)
    z = _dot(f_low, wf2_ref[...]) + vec(V_GLA_BF2, GLA_KEY)
    hbp = _segment_major(hb)
    xl = _dot(hbp, wxl_ref[...])
    xl_ref[LRU_HALO:LRU_HALO + T, :] = xl
    _fill_halo(xl_ref, xprev_ref, xl, LRU_HALO)
    gk = (jnp.minimum(z, 0.0) - jnp.log1p(jnp.exp(-jnp.abs(z)))) * (1.0 / GLA_GATE_NORM)

    row = lax.broadcasted_iota(jnp.int32, (T, T), 0)
    col = lax.broadcasted_iota(jnp.int32, (T, T), 1)
    intra = (row >= col) & ((row // GLA_CHUNK) == (col // GLA_CHUNK))
    tri = jnp.where(intra, 1.0, 0.0).astype(BF16)
    g1 = gk.astype(BF16)
    r1 = gk - g1.astype(F32)
    g2 = r1.astype(BF16)
    g3 = (r1 - g2.astype(F32)).astype(BF16)
    b = _dot(tri, g1) + _dot(tri, g2) + _dot(tri, g3)
    q = _dot(hb, wqkvg_ref[:, O_Q:O_K]) * (GLA_DK ** -0.5)
    k = _dot(hb, wqkvg_ref[:, O_K:O_V])
    v = _dot(hb, wqkvg_ref[:, O_V:O_G]).astype(BF16)

    xc = vec(V_LRU_CONV_B) + xl * vec(V_LRU_CONV_W + LRU_CONV - 1)
    for d in range(1, LRU_CONV):
        rows = slice(LRU_HALO - d * SUBLANES, LRU_HALO - d * SUBLANES + T)
        xc = xc + xl_ref[rows, :] * vec(V_LRU_CONV_W + LRU_CONV - 1 - d)
    xcb = xc.astype(BF16)
    row0 = 0
    r_parts, i_parts = [], []
    for (k0, k1, c0, c1) in BAND_TILES:
        both = _dot(xcb[:, k0:k1], wband_ref[row0:row0 + (k1 - k0), :])
        r_parts.append(both[:, :LANES])
        i_parts.append(both[:, LANES:])
        row0 += k1 - k0

    b_last_rows = [b[(c + 1) * GLA_CHUNK - 1:(c + 1) * GLA_CHUNK, :] for c in range(n_chunks)]
    b_last = jnp.concatenate([jnp.broadcast_to(r, (GLA_CHUNK, GLA_KEY)) for r in b_last_rows], axis=0)
    decay = [jnp.exp(r) for r in b_last_rows]
    q_t = (q * jnp.exp(b)).astype(BF16)
    k_t = (k * jnp.exp(-b)).astype(BF16)
    k_s = (k * jnp.exp(b_last - b)).astype(BF16)

    def chunk_expand(m):
        cols = []
        for c in range(n_chunks):
            parts = []
            if c > 0:
                parts.append(jnp.zeros((c * GLA_CHUNK, GLA_DK), m.dtype))
            parts.append(m[c * GLA_CHUNK:(c + 1) * GLA_CHUNK, :])
            if c < n_chunks - 1:
                parts.append(jnp.zeros((T - (c + 1) * GLA_CHUNK, GLA_DK), m.dtype))
            cols.append(jnp.concatenate(parts, axis=0))
        return jnp.concatenate(cols, axis=1)

    for hh in range(GLA_HEADS):
        ks_ = slice(hh * GLA_DK, (hh + 1) * GLA_DK)
        vs_ = slice(hh * GLA_DV, (hh + 1) * GLA_DV)
        qh = q_t[:, ks_]
        vh = v[:, vs_]
        a = jnp.where(intra, _dot_nt(qh, k_t[:, ks_]), 0.0).astype(BF16)
        ds_all = _dot_tn(vh, chunk_expand(k_s[:, ks_]))
        st = st_ref[hh]
        starts = []
        for c in range(n_chunks):
            starts.append(st.astype(BF16))
            st = st * decay[c][:, ks_] + ds_all[:, c * GLA_DK:(c + 1) * GLA_DK]
        st_ref[hh] = st
        o_intra = _dot(a, vh)
        for c in range(n_chunks):
            rs = slice(c * GLA_CHUNK, (c + 1) * GLA_CHUNK)
            o_ref[rs, vs_] = o_intra[rs, :] + _dot_nt(qh[rs, :], starts[c])

    g = _dot(hb, wqkvg_ref[:, O_G:O_F])
    lg = _dot(hbp, wlg_ref[...])
    gb = _dot(hbp, wgab_ref[:, D_MODEL:])
    ga = _dot(hb, wgab_ref[:, :D_MODEL])

    r_gate = _sigmoid(jnp.concatenate(r_parts, axis=1) + vec(V_LRU_BA))
    i_gate = _sigmoid(jnp.concatenate(i_parts, axis=1) + vec(V_LRU_BX))
    log_a = (-LRU_C) * r_gate * _softplus(vec(V_LRU_A_PARAM))
    a_s = jnp.exp(log_a)
    u_s = jnp.sqrt(-jnp.tanh(log_a) * (1.0 + a_s * a_s)) * (i_gate * xc)

    h_loc = u_s[0:SUBLANES, :]
    a_cum = a_s[0:SUBLANES, :]
    hl_ref[0:SUBLANES, :] = h_loc
    ac_ref[0:SUBLANES, :] = a_cum
    for r in range(1, seg_len):
        rs = slice(r * SUBLANES, (r + 1) * SUBLANES)
        h_loc = a_s[rs, :] * h_loc + u_s[rs, :]
        a_cum = a_s[rs, :] * a_cum
        hl_ref[rs, :] = h_loc
        ac_ref[rs, :] = a_cum
    sub = lax.broadcasted_iota(jnp.int32, (SUBLANES, LRU_PAD), 0)
    e_s, p_s = h_loc, a_cum
    s = 1
    while s < SUBLANES:
        m = sub >= s
        e_s = jnp.where(m, e_s + p_s * pltpu.roll(e_s, s, 0), e_s)
        p_s = jnp.where(m, p_s * pltpu.roll(p_s, s, 0), p_s)
        s *= 2
    h_carry = hc_ref[...]
    seg_end = e_s + p_s * h_carry
    seg_start = jnp.where(sub == 0, h_carry, pltpu.roll(seg_end, 1, 0))
    hc_ref[...] = seg_end[SUBLANES - 1:SUBLANES, :]
    for r in range(seg_len):
        rs = slice(r * SUBLANES, (r + 1) * SUBLANES)
        hl_ref[rs, :] = hl_ref[rs, :] + ac_ref[rs, :] * seg_start

    og = []
    for hh in range(GLA_HEADS):
        vs_ = slice(hh * GLA_DV, (hh + 1) * GLA_DV)
        gh = g[:, vs_]
        og.append(_rms_norm(o_ref[:, vs_], vec(V_GLA_ONORM, GLA_DV)) * (gh * _sigmoid(gh)))
    y_a = _dot(jnp.concatenate(og, axis=1).astype(BF16), gwo_ref[...])

    y_b = _dot((hl_ref[...] * _gelu_tanh(lg)).astype(BF16), lwo_ref[...])
    gated_a = _sigmoid(ga) * y_a
    _to_token_major(_sigmoid(gb) * y_b, slab_ref, mb_ref)

    merged = (gated_a + mb_ref[...]).astype(BF16)
    out_ref[...] = x + _dot(merged, wout_ref[...])


def _ffn_kernel(x_ref, vec_ref, wup_ref, wdown_ref, out_ref,
                u_ref, uprev_ref, slab_ref, y_ref):
    T = x_ref.shape[0]

    @pl.when(pl.program_id(0) == 0)
    def _():
        uprev_ref[...] = jnp.zeros_like(uprev_ref)

    def vec(r, n=2 * D_FF):
        return vec_ref[r:r + 1, :n]

    n_sub = u_ref.shape[0]
    H = T // n_sub
    us = []
    for sb in range(n_sub):
        x = x_ref[sb * H:(sb + 1) * H, :]
        hbp = _segment_major(_rms_norm(x, vec(V_FFN_NORM, D_MODEL)).astype(BF16))
        u = _dot(hbp, wup_ref[...])
        u_ref[sb, FFN_HALO:FFN_HALO + H, :] = u
        _fill_halo(u_ref.at[sb], uprev_ref, u, FFN_HALO)
        us.append(u)
    for sb in range(n_sub):
        uc = vec(V_FFN_CONV_B) + us[sb] * vec(V_FFN_CONV_W + FFN_CONV - 1)
        for d in range(1, FFN_CONV):
            rows = slice(FFN_HALO - d * SUBLANES, FFN_HALO - d * SUBLANES + H)
            uc = uc + u_ref[sb, rows, :] * vec(V_FFN_CONV_W + FFN_CONV - 1 - d)
        act = (_gelu_tanh(uc[:, :D_FF]) * uc[:, D_FF:]).astype(BF16)
        _to_token_major(_dot(act, wdown_ref[...]), slab_ref.at[sb], y_ref, sb * H)
        rows = slice(sb * H, (sb + 1) * H)
        out_ref[rows, :] = _rms_norm(x_ref[rows, :] + y_ref[rows, :], vec(V_FINAL_NORM, D_MODEL))


def _const_spec(shape):
    nd = len(shape)
    return pl.BlockSpec(shape, lambda i, _nd=nd: (0,) * _nd, pipeline_mode=pl.Buffered(1))


def _pad_cols(w, n):
    return jnp.pad(w, ((0, 0), (0, n - w.shape[1])))


def _block_rows(w):
    rows = w.reshape(LRU_WIDTH, LRU_BW).astype(F32)
    return jnp.pad(rows, ((0, LRU_PAD - LRU_WIDTH), (0, MXU_DIM - LRU_BW)))


def _pack_rows(vectors, n):
    return jnp.concatenate([_pad_cols(v.reshape(-1, v.shape[-1]).astype(F32), n) for v in vectors], axis=0)


@jax.jit
def _forward(x, attn_norm, w_in, gla_wf2, gla_bf2, gla_onorm, gla_wo, lru_conv_w, lru_conv_b, lru_wa, lru_ba,
             lru_wx, lru_bx, lru_a_param, lru_wo, w_out, ffn_norm, ffn_wup, ffn_conv_w, ffn_conv_b, ffn_wdown,
             final_norm):
    B, S, D = x.shape
    assert B == 1 and D == D_MODEL and S % SEQ_BLOCK == 0 and attn_norm.shape[0] == 1
    T = SEQ_BLOCK
    x2 = x.reshape(S, D)

    wi = w_in[0]
    w_qkvg = wi[:, O_Q:O_F].astype(BF16)
    w_f = _pad_cols(wi[:, O_F:O_XL], RANK_PAD).astype(BF16)
    w_xl = wi[:, O_XL:O_XL + LRU_PAD].astype(BF16)
    w_lg = wi[:, O_LG:O_LG + LRU_PAD].astype(BF16)
    w_gab = wi[:, O_GA:O_END].astype(BF16)
    wf2 = jnp.pad(gla_wf2[0], ((0, RANK_PAD - GLA_GATE_RANK), (0, 0)))
    lwo = jnp.pad(lru_wo[0], ((0, LRU_PAD - LRU_WIDTH), (0, 0)))

    mixer_vec = _pack_rows([attn_norm[0], gla_bf2[0], gla_onorm[0], lru_conv_b[0], lru_ba[0], lru_bx[0],
                            lru_a_param[0], lru_conv_w[0]], LRU_PAD)
    mixer_in = [
        x2, mixer_vec, w_qkvg, w_f, w_xl, w_lg, w_gab, wf2.astype(BF16), gla_wo[0].astype(BF16),
        _block_rows(lru_wa[0]), _block_rows(lru_wx[0]), lwo.astype(BF16), w_out[0].astype(BF16),
    ]
    assert S % (MIX_SUB * T) == 0
    blk = pl.BlockSpec((MIX_SUB * T, D), lambda i: (i, 0))
    params = pltpu.CompilerParams(dimension_semantics=("arbitrary",), vmem_limit_bytes=VMEM_LIMIT_BYTES)
    x1 = pl.pallas_call(
        _mixer_kernel,
        grid=(S // (MIX_SUB * T),),
        in_specs=[blk] + [_const_spec(a.shape) for a in mixer_in[1:]],
        out_specs=blk,
        out_shape=jax.ShapeDtypeStruct((S, D), F32),
        scratch_shapes=[
            pltpu.VMEM((GLA_HEADS, GLA_DV, GLA_DK), F32),
            pltpu.VMEM((LRU_HALO + T, LRU_PAD), F32),
            pltpu.VMEM((LRU_HALO, LRU_PAD), F32),
            pltpu.VMEM((1, LRU_PAD), F32),
            pltpu.VMEM((T, GLA_VAL), F32),
            pltpu.VMEM((T, LRU_PAD), F32),
            pltpu.VMEM((T, LRU_PAD), F32),
            pltpu.VMEM((D // LANES, T, LANES), F32),
            pltpu.VMEM((T, D), F32),
            pltpu.VMEM((BAND_ROWS, 2 * LANES), BF16),
        ],
        compiler_params=params,
        name="mixer",
    )(*mixer_in)

    ffn_vec = _pack_rows([ffn_norm[0], final_norm, ffn_conv_b[0], ffn_conv_w[0]], 2 * D_FF)
    ffn_in = [x1, ffn_vec, ffn_wup[0].astype(BF16), ffn_wdown[0].astype(BF16)]
    assert S % (FFN_SUB * T) == 0
    ffn_blk = pl.BlockSpec((FFN_SUB * T, D), lambda i: (i, 0))
    out = pl.pallas_call(
        _ffn_kernel,
        grid=(S // (FFN_SUB * T),),
        in_specs=[ffn_blk] + [_const_spec(a.shape) for a in ffn_in[1:]],
        out_specs=ffn_blk,
        out_shape=jax.ShapeDtypeStruct((S, D), F32),
        scratch_shapes=[
            pltpu.VMEM((FFN_SUB, FFN_HALO + T, 2 * D_FF), F32),
            pltpu.VMEM((FFN_HALO, 2 * D_FF), F32),
            pltpu.VMEM((FFN_SUB, D // LANES, T, LANES), F32),
            pltpu.VMEM((FFN_SUB * T, D), F32),
        ],
        compiler_params=params,
        name="ffn",
    )(*ffn_in)
    return out.reshape(B, S, D)


def kernel(x, attn_norm, w_in, gla_wf2, gla_bf2, gla_onorm, gla_wo, lru_conv_w, lru_conv_b, lru_wa, lru_ba, lru_wx, lru_bx, lru_a_param, lru_wo, w_out, ffn_norm, ffn_wup, ffn_conv_w, ffn_conv_b, ffn_wdown, final_norm):
    return _forward(x, attn_norm, w_in, gla_wf2, gla_bf2, gla_onorm, gla_wo, lru_conv_w, lru_conv_b, lru_wa, lru_ba,
                    lru_wx, lru_bx, lru_a_param, lru_wo, w_out, ffn_norm, ffn_wup, ffn_conv_w, ffn_conv_b,
                    ffn_wdown, final_norm)
```

```python
import jax
import jax.numpy as jnp
import numpy as np
from jax import lax
from jax.experimental import pallas as pl
from jax.experimental.pallas import tpu as pltpu

D_MODEL = 1024
GLA_HEADS = 4
GLA_DK = 128
GLA_DV = 256
GLA_KEY = GLA_HEADS * GLA_DK
GLA_VAL = GLA_HEADS * GLA_DV
GLA_GATE_RANK = 16
GLA_GATE_NORM = 16.0
GLA_CHUNK = 64
LRU_WIDTH = 1344
LRU_BLOCKS = 8
LRU_BW = LRU_WIDTH // LRU_BLOCKS
LRU_CONV = 4
LRU_C = 8.0
D_FF = 3 * D_MODEL
FFN_CONV = 3
EPS = 1e-6

LANES = 128
SUBLANES = 8
MXU_DIM = 256
V7X_VMEM_BYTES = 64 * 1024 * 1024
VMEM_LIMIT_BYTES = V7X_VMEM_BYTES - 6 * 1024 * 1024

LRU_PAD = ((LRU_WIDTH + LANES - 1) // LANES) * LANES
RANK_PAD = LANES
SEQ_BLOCK = 256
LRU_HALO = (LRU_CONV - 1) * SUBLANES
FFN_HALO = (FFN_CONV - 1) * SUBLANES
FFN_SUB = 2

(O_Q, O_K, O_V, O_G, O_F, O_XL, O_LG, O_GA, O_GB, O_END) = [int(v) for v in np.cumsum(
    [0, GLA_KEY, GLA_KEY, GLA_VAL, GLA_VAL, GLA_GATE_RANK, LRU_WIDTH, LRU_WIDTH, D_MODEL, D_MODEL])]

(V_ATTN_NORM, V_GLA_BF2, V_GLA_ONORM, V_LRU_CONV_B, V_LRU_BA, V_LRU_BX, V_LRU_A_PARAM, V_LRU_CONV_W) = range(8)
(V_FFN_NORM, V_FINAL_NORM, V_FFN_CONV_B, V_FFN_CONV_W) = range(4)

BF16 = jnp.bfloat16
F32 = jnp.float32


def _band_tiles():
    tiles = []
    for c0 in range(0, LRU_PAD, LANES):
        c1 = min(c0 + LANES, LRU_PAD)
        last_col = min(c1, LRU_WIDTH) - 1
        b_lo = c0 // LRU_BW
        b_hi = last_col // LRU_BW
        k0 = (b_lo * LRU_BW) // LANES * LANES
        k1 = min(-(-((b_hi + 1) * LRU_BW) // LANES) * LANES, LRU_PAD)
        tiles.append((k0, k1, c0, c1))
    return tuple(tiles)


BAND_TILES = _band_tiles()
BAND_ROWS = sum(k1 - k0 for k0, k1, _, _ in BAND_TILES)


def _dot(a, b):
    return jnp.dot(a, b, preferred_element_type=F32)


def _dot_nt(a, b):
    return lax.dot_general(a, b, (((1,), (1,)), ((), ())), preferred_element_type=F32)


def _dot_tn(a, b):
    return lax.dot_general(a, b, (((0,), (0,)), ((), ())), preferred_element_type=F32)


def _sigmoid(x):
    return 1.0 / (1.0 + jnp.exp(-x))


def _softplus(x):
    return jnp.maximum(x, 0.0) + jnp.log1p(jnp.exp(-jnp.abs(x)))


def _gelu_tanh(x):
    c = float(np.sqrt(2.0 / np.pi))
    hx = 0.5 * x
    return hx + hx * jnp.tanh(x * (c + (0.044715 * c) * (x * x)))


def _rms_norm(x, g):
    ms = jnp.mean(x * x, axis=-1, keepdims=True)
    return x * lax.rsqrt(ms + EPS) * g


def _segment_major(hb):
    T = hb.shape[0]
    seg_len = T // SUBLANES
    p = lax.broadcasted_iota(jnp.int32, (T, T), 0)
    t = lax.broadcasted_iota(jnp.int32, (T, T), 1)
    perm = jnp.where(t == (p % SUBLANES) * seg_len + p // SUBLANES, 1.0, 0.0).astype(BF16)
    return _dot(perm, hb).astype(BF16)


def _fill_halo(buf_ref, prev_ref, cur, halo):
    T = cur.shape[0]
    sub = lax.broadcasted_iota(jnp.int32, (SUBLANES, cur.shape[1]), 0)
    for off in range(0, halo, SUBLANES):
        rows = slice(off, off + SUBLANES)
        tail = cur[T - halo + off:T - halo + off + SUBLANES, :]
        buf_ref[rows, :] = pltpu.roll(jnp.where(sub == SUBLANES - 1, prev_ref[rows, :], tail), 1, 0)
    prev_ref[...] = cur[T - halo:T, :]


def _to_token_major(val, slab_ref, dst_ref, row0=0):
    T, d = val.shape
    seg_len = T // SUBLANES
    for n in range(d // LANES):
        slab_ref[n] = val[:, n * LANES:(n + 1) * LANES]
    for j in range(T // SUBLANES):
        t0 = j * SUBLANES
        start = (t0 % seg_len) * SUBLANES + t0 // seg_len
        for n in range(d // LANES):
            dst_ref[row0 + t0:row0 + t0 + SUBLANES, n * LANES:(n + 1) * LANES] = (
                slab_ref[n, pl.ds(start, SUBLANES, stride=SUBLANES), :])


def _build_band(w_refs, wband_ref):
    src = lax.broadcasted_iota(jnp.int32, (MXU_DIM, LANES), 0)
    dst = lax.broadcasted_iota(jnp.int32, (MXU_DIM, LANES), 1)
    row0 = 0
    for (k0, k1, c0, c1) in BAND_TILES:
        row_block = (lax.broadcasted_iota(jnp.int32, (k1 - k0, MXU_DIM), 0) + k0) // LRU_BW
        for gate, w_ref in enumerate(w_refs):
            rows = w_ref[k0:k1, :]
            tile = jnp.zeros((k1 - k0, LANES), F32)
            for n in range(c0 // LRU_BW, min((c1 - 1) // LRU_BW, LRU_BLOCKS - 1) + 1):
                place = jnp.where(src == dst + (c0 - n * LRU_BW), 1.0, 0.0).astype(BF16)
                tile = tile + _dot(jnp.where(row_block == n, rows, 0.0).astype(BF16), place)
            wband_ref[row0:row0 + (k1 - k0), gate * LANES:(gate + 1) * LANES] = tile.astype(BF16)
        row0 += k1 - k0


def _mixer_kernel(x_ref, vec_ref, wqkvg_ref, wf_ref, wxl_ref, wlg_ref, wgab_ref, wf2_ref, gwo_ref,
                  wa_ref, wx_ref, lwo_ref, wout_ref,
                  out_ref,
                  st_ref, xl_ref, xprev_ref, hc_ref, o_ref, hl_ref, ac_ref, slab_ref, mb_ref, wband_ref):
    T = x_ref.shape[0]
    n_chunks = T // GLA_CHUNK
    seg_len = T // SUBLANES

    @pl.when(pl.program_id(0) == 0)
    def _():
        st_ref[...] = jnp.zeros_like(st_ref)
        xprev_ref[...] = jnp.zeros_like(xprev_ref)
        hc_ref[...] = jnp.zeros_like(hc_ref)
        _build_band((wa_ref, wx_ref), wband_ref)

    def vec(r, n=LRU_PAD):
        return vec_ref[r:r + 1, :n]

    x = x_ref[...]
    hb = _rms_norm(x, vec(V_ATTN_NORM, D_MODEL)).astype(BF16)

    f_low = _dot(hb, wf_ref[...]).astype(BF16)
    z = _dot(f_low, wf2_ref[...]) + vec(V_GLA_BF2, GLA_KEY)
    hbp = _segment_major(hb)
    xl = _dot(hbp, wxl_ref[...])
    xl_ref[LRU_HALO:LRU_HALO + T, :] = xl
    _fill_halo(xl_ref, xprev_ref, xl, LRU_HALO)
    gk = (jnp.minimum(z, 0.0) - jnp.log1p(jnp.exp(-jnp.abs(z)))) * (1.0 / GLA_GATE_NORM)

    row = lax.broadcasted_iota(jnp.int32, (T, T), 0)
    col = lax.broadcasted_iota(jnp.int32, (T, T), 1)
    intra = (row >= col) & ((row // GLA_CHUNK) == (col // GLA_CHUNK))
    tri = jnp.where(intra, 1.0, 0.0).astype(BF16)
    g1 = gk.astype(BF16)
    r1 = gk - g1.astype(F32)
    g2 = r1.astype(BF16)
    g3 = (r1 - g2.astype(F32)).astype(BF16)
    b = _dot(tri, g1) + _dot(tri, g2) + _dot(tri, g3)
    q = _dot(hb, wqkvg_ref[:, O_Q:O_K]) * (GLA_DK ** -0.5)
    k = _dot(hb, wqkvg_ref[:, O_K:O_V])
    v = _dot(hb, wqkvg_ref[:, O_V:O_G]).astype(BF16)

    xc = vec(V_LRU_CONV_B) + xl * vec(V_LRU_CONV_W + LRU_CONV - 1)
    for d in range(1, LRU_CONV):
        rows = slice(LRU_HALO - d * SUBLANES, LRU_HALO - d * SUBLANES + T)
        xc = xc + xl_ref[rows, :] * vec(V_LRU_CONV_W + LRU_CONV - 1 - d)
    xcb = xc.astype(BF16)
    row0 = 0
    r_parts, i_parts = [], []
    for (k0, k1, c0, c1) in BAND_TILES:
        both = _dot(xcb[:, k0:k1], wband_ref[row0:row0 + (k1 - k0), :])
        r_parts.append(both[:, :LANES])
        i_parts.append(both[:, LANES:])
        row0 += k1 - k0

    b_last_rows = [b[(c + 1) * GLA_CHUNK - 1:(c + 1) * GLA_CHUNK, :] for c in range(n_chunks)]
    b_last = jnp.concatenate([jnp.broadcast_to(r, (GLA_CHUNK, GLA_KEY)) for r in b_last_rows], axis=0)
    decay = [jnp.exp(r) for r in b_last_rows]
    q_t = (q * jnp.exp(b)).astype(BF16)
    k_t = (k * jnp.exp(-b)).astype(BF16)
    k_s = (k * jnp.exp(b_last - b)).astype(BF16)

    def chunk_expand(m):
        cols = []
        for c in range(n_chunks):
            parts = []
            if c > 0:
                parts.append(jnp.zeros((c * GLA_CHUNK, GLA_DK), m.dtype))
            parts.append(m[c * GLA_CHUNK:(c + 1) * GLA_CHUNK, :])
            if c < n_chunks - 1:
                parts.append(jnp.zeros((T - (c + 1) * GLA_CHUNK, GLA_DK), m.dtype))
            cols.append(jnp.concatenate(parts, axis=0))
        return jnp.concatenate(cols, axis=1)

    for hh in range(GLA_HEADS):
        ks_ = slice(hh * GLA_DK, (hh + 1) * GLA_DK)
        vs_ = slice(hh * GLA_DV, (hh + 1) * GLA_DV)
        qh = q_t[:, ks_]
        vh = v[:, vs_]
        a = jnp.where(intra, _dot_nt(qh, k_t[:, ks_]), 0.0).astype(BF16)
        ds_all = _dot_tn(vh, chunk_expand(k_s[:, ks_]))
        st = st_ref[hh]
        starts = []
        for c in range(n_chunks):
            starts.append(st.astype(BF16))
            st = st * decay[c][:, ks_] + ds_all[:, c * GLA_DK:(c + 1) * GLA_DK]
        st_ref[hh] = st
        o_ref[:, vs_] = _dot(a, vh) + _dot_nt(chunk_expand(qh), jnp.concatenate(starts, axis=1))

    g = _dot(hb, wqkvg_ref[:, O_G:O_F])
    lg = _dot(hbp, wlg_ref[...])
    gb = _dot(hbp, wgab_ref[:, D_MODEL:])
    ga = _dot(hb, wgab_ref[:, :D_MODEL])

    r_gate = _sigmoid(jnp.concatenate(r_parts, axis=1) + vec(V_LRU_BA))
    i_gate = _sigmoid(jnp.concatenate(i_parts, axis=1) + vec(V_LRU_BX))
    log_a = (-LRU_C) * r_gate * _softplus(vec(V_LRU_A_PARAM))
    a_s = jnp.exp(log_a)
    u_s = jnp.sqrt(-jnp.tanh(log_a) * (1.0 + a_s * a_s)) * (i_gate * xc)

    h_loc = u_s[0:SUBLANES, :]
    a_cum = a_s[0:SUBLANES, :]
    hl_ref[0:SUBLANES, :] = h_loc
    ac_ref[0:SUBLANES, :] = a_cum
    for r in range(1, seg_len):
        rs = slice(r * SUBLANES, (r + 1) * SUBLANES)
        h_loc = a_s[rs, :] * h_loc + u_s[rs, :]
        a_cum = a_s[rs, :] * a_cum
        hl_ref[rs, :] = h_loc
        ac_ref[rs, :] = a_cum
    sub = lax.broadcasted_iota(jnp.int32, (SUBLANES, LRU_PAD), 0)
    e_s, p_s = h_loc, a_cum
    s = 1
    while s < SUBLANES:
        m = sub >= s
        e_s = jnp.where(m, e_s + p_s * pltpu.roll(e_s, s, 0), e_s)
        p_s = jnp.where(m, p_s * pltpu.roll(p_s, s, 0), p_s)
        s *= 2
    h_carry = hc_ref[...]
    seg_end = e_s + p_s * h_carry
    seg_start = jnp.where(sub == 0, h_carry, pltpu.roll(seg_end, 1, 0))
    hc_ref[...] = seg_end[SUBLANES - 1:SUBLANES, :]
    for r in range(seg_len):
        rs = slice(r * SUBLANES, (r + 1) * SUBLANES)
        hl_ref[rs, :] = hl_ref[rs, :] + ac_ref[rs, :] * seg_start

    og = []
    for hh in range(GLA_HEADS):
        vs_ = slice(hh * GLA_DV, (hh + 1) * GLA_DV)
        gh = g[:, vs_]
        og.append(_rms_norm(o_ref[:, vs_], vec(V_GLA_ONORM, GLA_DV)) * (gh * _sigmoid(gh)))
    y_a = _dot(jnp.concatenate(og, axis=1).astype(BF16), gwo_ref[...])

    y_b = _dot((hl_ref[...] * _gelu_tanh(lg)).astype(BF16), lwo_ref[...])
    gated_a = _sigmoid(ga) * y_a
    _to_token_major(_sigmoid(gb) * y_b, slab_ref, mb_ref)

    merged = (gated_a + mb_ref[...]).astype(BF16)
    out_ref[...] = x + _dot(merged, wout_ref[...])


def _ffn_kernel(x_ref, vec_ref, wup_ref, wdown_ref, out_ref,
                u_ref, uprev_ref, slab_ref, y_ref):
    T = x_ref.shape[0]

    @pl.when(pl.program_id(0) == 0)
    def _():
        uprev_ref[...] = jnp.zeros_like(uprev_ref)

    def vec(r, n=2 * D_FF):
        return vec_ref[r:r + 1, :n]

    n_sub = u_ref.shape[0]
    H = T // n_sub
    us = []
    for sb in range(n_sub):
        x = x_ref[sb * H:(sb + 1) * H, :]
        hbp = _segment_major(_rms_norm(x, vec(V_FFN_NORM, D_MODEL)).astype(BF16))
        u = _dot(hbp, wup_ref[...])
        u_ref[sb, FFN_HALO:FFN_HALO + H, :] = u
        _fill_halo(u_ref.at[sb], uprev_ref, u, FFN_HALO)
        us.append(u)
    for sb in range(n_sub):
        uc = vec(V_FFN_CONV_B) + us[sb] * vec(V_FFN_CONV_W + FFN_CONV - 1)
        for d in range(1, FFN_CONV):
            rows = slice(FFN_HALO - d * SUBLANES, FFN_HALO - d * SUBLANES + H)
            uc = uc + u_ref[sb, rows, :] * vec(V_FFN_CONV_W + FFN_CONV - 1 - d)
        act = (_gelu_tanh(uc[:, :D_FF]) * uc[:, D_FF:]).astype(BF16)
        _to_token_major(_dot(act, wdown_ref[...]), slab_ref.at[sb], y_ref, sb * H)
        rows = slice(sb * H, (sb + 1) * H)
        out_ref[rows, :] = _rms_norm(x_ref[rows, :] + y_ref[rows, :], vec(V_FINAL_NORM, D_MODEL))


def _const_spec(shape):
    nd = len(shape)
    return pl.BlockSpec(shape, lambda i, _nd=nd: (0,) * _nd, pipeline_mode=pl.Buffered(1))


def _pad_cols(w, n):
    return jnp.pad(w, ((0, 0), (0, n - w.shape[1])))


def _block_rows(w):
    rows = w.reshape(LRU_WIDTH, LRU_BW).astype(F32)
    return jnp.pad(rows, ((0, LRU_PAD - LRU_WIDTH), (0, MXU_DIM - LRU_BW)))


def _pack_rows(vectors, n):
    return jnp.concatenate([_pad_cols(v.reshape(-1, v.shape[-1]).astype(F32), n) for v in vectors], axis=0)


@jax.jit
def _forward(x, attn_norm, w_in, gla_wf2, gla_bf2, gla_onorm, gla_wo, lru_conv_w, lru_conv_b, lru_wa, lru_ba,
             lru_wx, lru_bx, lru_a_param, lru_wo, w_out, ffn_norm, ffn_wup, ffn_conv_w, ffn_conv_b, ffn_wdown,
             final_norm):
    B, S, D = x.shape
    assert B == 1 and D == D_MODEL and S % SEQ_BLOCK == 0 and attn_norm.shape[0] == 1
    T = SEQ_BLOCK
    x2 = x.reshape(S, D)

    wi = w_in[0]
    w_qkvg = wi[:, O_Q:O_F].astype(BF16)
    w_f = _pad_cols(wi[:, O_F:O_XL], RANK_PAD).astype(BF16)
    w_xl = wi[:, O_XL:O_XL + LRU_PAD].astype(BF16)
    w_lg = wi[:, O_LG:O_LG + LRU_PAD].astype(BF16)
    w_gab = wi[:, O_GA:O_END].astype(BF16)
    wf2 = jnp.pad(gla_wf2[0], ((0, RANK_PAD - GLA_GATE_RANK), (0, 0)))
    lwo = jnp.pad(lru_wo[0], ((0, LRU_PAD - LRU_WIDTH), (0, 0)))

    mixer_vec = _pack_rows([attn_norm[0], gla_bf2[0], gla_onorm[0], lru_conv_b[0], lru_ba[0], lru_bx[0],
                            lru_a_param[0], lru_conv_w[0]], LRU_PAD)
    mixer_in = [
        x2, mixer_vec, w_qkvg, w_f, w_xl, w_lg, w_gab, wf2.astype(BF16), gla_wo[0].astype(BF16),
        _block_rows(lru_wa[0]), _block_rows(lru_wx[0]), lwo.astype(BF16), w_out[0].astype(BF16),
    ]
    blk = pl.BlockSpec((T, D), lambda i: (i, 0))
    params = pltpu.CompilerParams(dimension_semantics=("arbitrary",), vmem_limit_bytes=VMEM_LIMIT_BYTES)
    x1 = pl.pallas_call(
        _mixer_kernel,
        grid=(S // T,),
        in_specs=[blk] + [_const_spec(a.shape) for a in mixer_in[1:]],
        out_specs=blk,
        out_shape=jax.ShapeDtypeStruct((S, D), F32),
        scratch_shapes=[
            pltpu.VMEM((GLA_HEADS, GLA_DV, GLA_DK), F32),
            pltpu.VMEM((LRU_HALO + T, LRU_PAD), F32),
            pltpu.VMEM((LRU_HALO, LRU_PAD), F32),
            pltpu.VMEM((1, LRU_PAD), F32),
            pltpu.VMEM((T, GLA_VAL), F32),
            pltpu.VMEM((T, LRU_PAD), F32),
            pltpu.VMEM((T, LRU_PAD), F32),
            pltpu.VMEM((D // LANES, T, LANES), F32),
            pltpu.VMEM((T, D), F32),
            pltpu.VMEM((BAND_ROWS, 2 * LANES), BF16),
        ],
        compiler_params=params,
        name="mixer",
    )(*mixer_in)

    ffn_vec = _pack_rows([ffn_norm[0], final_norm, ffn_conv_b[0], ffn_conv_w[0]], 2 * D_FF)
    ffn_in = [x1, ffn_vec, ffn_wup[0].astype(BF16), ffn_wdown[0].astype(BF16)]
    assert S % (FFN_SUB * T) == 0
    ffn_blk = pl.BlockSpec((FFN_SUB * T, D), lambda i: (i, 0))
    out = pl.pallas_call(
        _ffn_kernel,
        grid=(S // (FFN_SUB * T),),
        in_specs=[ffn_blk] + [_const_spec(a.shape) for a in ffn_in[1:]],
        out_specs=ffn_blk,
        out_shape=jax.ShapeDtypeStruct((S, D), F32),
        scratch_shapes=[
            pltpu.VMEM((FFN_SUB, FFN_HALO + T, 2 * D_FF), F32),
            pltpu.VMEM((FFN_HALO, 2 * D_FF), F32),
            pltpu.VMEM((FFN_SUB, D // LANES, T, LANES), F32),
            pltpu.VMEM((FFN_SUB * T, D), F32),
        ],
        compiler_params=params,
        name="ffn",
    )(*ffn_in)
    return out.reshape(B, S, D)


def kernel(x, attn_norm, w_in, gla_wf2, gla_bf2, gla_onorm, gla_wo, lru_conv_w, lru_conv_b, lru_wa, lru_ba, lru_wx, lru_bx, lru_a_param, lru_wo, w_out, ffn_norm, ffn_wup, ffn_conv_w, ffn_conv_b, ffn_wdown, final_norm):
    return _forward(x, attn_norm, w_in, gla_wf2, gla_bf2, gla_onorm, gla_wo, lru_conv_w, lru_conv_b, lru_wa, lru_ba,
                    lru_wx, lru_bx, lru_a_param, lru_wo, w_out, ffn_norm, ffn_wup, ffn_conv_w, ffn_conv_b,
                    ffn_wdown, final_norm)
```

```python
import jax
import jax.numpy as jnp
import numpy as np
from jax import lax
from jax.experimental import pallas as pl
from jax.experimental.pallas import tpu as pltpu

D_MODEL = 1024
GLA_HEADS = 4
GLA_DK = 128
GLA_DV = 256
GLA_KEY = GLA_HEADS * GLA_DK
GLA_VAL = GLA_HEADS * GLA_DV
GLA_GATE_RANK = 16
GLA_GATE_NORM = 16.0
GLA_CHUNK = 64
LRU_WIDTH = 1344
LRU_BLOCKS = 8
LRU_BW = LRU_WIDTH // LRU_BLOCKS
LRU_CONV = 4
LRU_C = 8.0
D_FF = 3 * D_MODEL
FFN_CONV = 3
EPS = 1e-6

LANES = 128
SUBLANES = 8
MXU_DIM = 256
V7X_VMEM_BYTES = 64 * 1024 * 1024
VMEM_LIMIT_BYTES = V7X_VMEM_BYTES - 6 * 1024 * 1024

LRU_PAD = ((LRU_WIDTH + LANES - 1) // LANES) * LANES
RANK_PAD = LANES
SEQ_BLOCK = 256
LRU_HALO = (LRU_CONV - 1) * SUBLANES
FFN_HALO = (FFN_CONV - 1) * SUBLANES
FFN_SUB = 2
WEIGHT_CHUNKS = 8

(O_Q, O_K, O_V, O_G, O_F, O_XL, O_LG, O_GA, O_GB, O_END) = [int(v) for v in np.cumsum(
    [0, GLA_KEY, GLA_KEY, GLA_VAL, GLA_VAL, GLA_GATE_RANK, LRU_WIDTH, LRU_WIDTH, D_MODEL, D_MODEL])]

(V_ATTN_NORM, V_GLA_BF2, V_GLA_ONORM, V_LRU_CONV_B, V_LRU_BA, V_LRU_BX, V_LRU_A_PARAM, V_LRU_CONV_W) = range(8)
(V_FFN_NORM, V_FINAL_NORM, V_FFN_CONV_B, V_FFN_CONV_W) = range(4)

BF16 = jnp.bfloat16
F32 = jnp.float32


def _band_tiles():
    tiles = []
    for c0 in range(0, LRU_PAD, LANES):
        c1 = min(c0 + LANES, LRU_PAD)
        last_col = min(c1, LRU_WIDTH) - 1
        b_lo = c0 // LRU_BW
        b_hi = last_col // LRU_BW
        k0 = (b_lo * LRU_BW) // LANES * LANES
        k1 = min(-(-((b_hi + 1) * LRU_BW) // LANES) * LANES, LRU_PAD)
        tiles.append((k0, k1, c0, c1))
    return tuple(tiles)


BAND_TILES = _band_tiles()
BAND_ROWS = sum(k1 - k0 for k0, k1, _, _ in BAND_TILES)


def _dot(a, b):
    return jnp.dot(a, b, preferred_element_type=F32)


def _dot_nt(a, b):
    return lax.dot_general(a, b, (((1,), (1,)), ((), ())), preferred_element_type=F32)


def _dot_tn(a, b):
    return lax.dot_general(a, b, (((0,), (0,)), ((), ())), preferred_element_type=F32)


def _sigmoid(x):
    return 1.0 / (1.0 + jnp.exp(-x))


def _softplus(x):
    return jnp.maximum(x, 0.0) + jnp.log1p(jnp.exp(-jnp.abs(x)))


def _gelu_tanh(x):
    c = float(np.sqrt(2.0 / np.pi))
    hx = 0.5 * x
    return hx + hx * jnp.tanh(x * (c + (0.044715 * c) * (x * x)))


def _rms_norm(x, g):
    ms = jnp.mean(x * x, axis=-1, keepdims=True)
    return x * lax.rsqrt(ms + EPS) * g


def _segment_major(hb):
    T = hb.shape[0]
    seg_len = T // SUBLANES
    p = lax.broadcasted_iota(jnp.int32, (T, T), 0)
    t = lax.broadcasted_iota(jnp.int32, (T, T), 1)
    perm = jnp.where(t == (p % SUBLANES) * seg_len + p // SUBLANES, 1.0, 0.0).astype(BF16)
    return _dot(perm, hb).astype(BF16)


def _fill_halo(buf_ref, prev_ref, cur, halo):
    T = cur.shape[0]
    sub = lax.broadcasted_iota(jnp.int32, (SUBLANES, cur.shape[1]), 0)
    for off in range(0, halo, SUBLANES):
        rows = slice(off, off + SUBLANES)
        tail = cur[T - halo + off:T - halo + off + SUBLANES, :]
        buf_ref[rows, :] = pltpu.roll(jnp.where(sub == SUBLANES - 1, prev_ref[rows, :], tail), 1, 0)
    prev_ref[...] = cur[T - halo:T, :]


def _to_token_major(val, slab_ref, dst_ref, row0=0):
    T, d = val.shape
    seg_len = T // SUBLANES
    for n in range(d // LANES):
        slab_ref[n] = val[:, n * LANES:(n + 1) * LANES]
    for j in range(T // SUBLANES):
        t0 = j * SUBLANES
        start = (t0 % seg_len) * SUBLANES + t0 // seg_len
        for n in range(d // LANES):
            dst_ref[row0 + t0:row0 + t0 + SUBLANES, n * LANES:(n + 1) * LANES] = (
                slab_ref[n, pl.ds(start, SUBLANES, stride=SUBLANES), :])


def _build_band(w_refs, wband_ref):
    src = lax.broadcasted_iota(jnp.int32, (MXU_DIM, LANES), 0)
    dst = lax.broadcasted_iota(jnp.int32, (MXU_DIM, LANES), 1)
    row0 = 0
    for (k0, k1, c0, c1) in BAND_TILES:
        row_block = (lax.broadcasted_iota(jnp.int32, (k1 - k0, MXU_DIM), 0) + k0) // LRU_BW
        for gate, w_ref in enumerate(w_refs):
            rows = w_ref[k0:k1, :]
            tile = jnp.zeros((k1 - k0, LANES), F32)
            for n in range(c0 // LRU_BW, min((c1 - 1) // LRU_BW, LRU_BLOCKS - 1) + 1):
                place = jnp.where(src == dst + (c0 - n * LRU_BW), 1.0, 0.0).astype(BF16)
                tile = tile + _dot(jnp.where(row_block == n, rows, 0.0).astype(BF16), place)
            wband_ref[row0:row0 + (k1 - k0), gate * LANES:(gate + 1) * LANES] = tile.astype(BF16)
        row0 += k1 - k0


def _mixer_kernel(x_ref, vec_ref, wqkvg_ref, wf_ref, wxl_ref, wlg_ref, wgab_ref, wf2_ref, gwo_ref,
                  wa_ref, wx_ref, lwo_ref, wout_ref,
                  out_ref,
                  st_ref, xl_ref, xprev_ref, hc_ref, o_ref, hl_ref, ac_ref, slab_ref, mb_ref, wband_ref):
    T = x_ref.shape[0]
    n_chunks = T // GLA_CHUNK
    seg_len = T // SUBLANES

    @pl.when(pl.program_id(0) == 0)
    def _():
        st_ref[...] = jnp.zeros_like(st_ref)
        xprev_ref[...] = jnp.zeros_like(xprev_ref)
        hc_ref[...] = jnp.zeros_like(hc_ref)
        _build_band((wa_ref, wx_ref), wband_ref)

    def vec(r, n=LRU_PAD):
        return vec_ref[r:r + 1, :n]

    x = x_ref[...]
    hb = _rms_norm(x, vec(V_ATTN_NORM, D_MODEL)).astype(BF16)

    f_low = _dot(hb, wf_ref[...]).astype(BF16)
    z = _dot(f_low, wf2_ref[...]) + vec(V_GLA_BF2, GLA_KEY)
    hbp = _segment_major(hb)
    xl = _dot(hbp, wxl_ref[...])
    xl_ref[LRU_HALO:LRU_HALO + T, :] = xl
    _fill_halo(xl_ref, xprev_ref, xl, LRU_HALO)
    gk = (jnp.minimum(z, 0.0) - jnp.log1p(jnp.exp(-jnp.abs(z)))) * (1.0 / GLA_GATE_NORM)

    row = lax.broadcasted_iota(jnp.int32, (T, T), 0)
    col = lax.broadcasted_iota(jnp.int32, (T, T), 1)
    intra = (row >= col) & ((row // GLA_CHUNK) == (col // GLA_CHUNK))
    tri = jnp.where(intra, 1.0, 0.0).astype(BF16)
    g1 = gk.astype(BF16)
    r1 = gk - g1.astype(F32)
    g2 = r1.astype(BF16)
    g3 = (r1 - g2.astype(F32)).astype(BF16)
    b = _dot(tri, g1) + _dot(tri, g2) + _dot(tri, g3)
    q = _dot(hb, wqkvg_ref[:, O_Q:O_K]) * (GLA_DK ** -0.5)
    k = _dot(hb, wqkvg_ref[:, O_K:O_V])
    v = _dot(hb, wqkvg_ref[:, O_V:O_G]).astype(BF16)

    xc = vec(V_LRU_CONV_B) + xl * vec(V_LRU_CONV_W + LRU_CONV - 1)
    for d in range(1, LRU_CONV):
        rows = slice(LRU_HALO - d * SUBLANES, LRU_HALO - d * SUBLANES + T)
        xc = xc + xl_ref[rows, :] * vec(V_LRU_CONV_W + LRU_CONV - 1 - d)
    xcb = xc.astype(BF16)
    row0 = 0
    r_parts, i_parts = [], []
    for (k0, k1, c0, c1) in BAND_TILES:
        both = _dot(xcb[:, k0:k1], wband_ref[row0:row0 + (k1 - k0), :])
        r_parts.append(both[:, :LANES])
        i_parts.append(both[:, LANES:])
        row0 += k1 - k0

    b_last_rows = [b[(c + 1) * GLA_CHUNK - 1:(c + 1) * GLA_CHUNK, :] for c in range(n_chunks)]
    b_last = jnp.concatenate([jnp.broadcast_to(r, (GLA_CHUNK, GLA_KEY)) for r in b_last_rows], axis=0)
    decay = [jnp.exp(r) for r in b_last_rows]
    q_t = (q * jnp.exp(b)).astype(BF16)
    k_t = (k * jnp.exp(-b)).astype(BF16)
    k_s = (k * jnp.exp(b_last - b)).astype(BF16)

    def chunk_expand(m):
        cols = []
        for c in range(n_chunks):
            parts = []
            if c > 0:
                parts.append(jnp.zeros((c * GLA_CHUNK, GLA_DK), m.dtype))
            parts.append(m[c * GLA_CHUNK:(c + 1) * GLA_CHUNK, :])
            if c < n_chunks - 1:
                parts.append(jnp.zeros((T - (c + 1) * GLA_CHUNK, GLA_DK), m.dtype))
            cols.append(jnp.concatenate(parts, axis=0))
        return jnp.concatenate(cols, axis=1)

    for hh in range(GLA_HEADS):
        ks_ = slice(hh * GLA_DK, (hh + 1) * GLA_DK)
        vs_ = slice(hh * GLA_DV, (hh + 1) * GLA_DV)
        qh = q_t[:, ks_]
        vh = v[:, vs_]
        a = jnp.where(intra, _dot_nt(qh, k_t[:, ks_]), 0.0).astype(BF16)
        ds_all = _dot_tn(vh, chunk_expand(k_s[:, ks_]))
        st = st_ref[hh]
        starts = []
        for c in range(n_chunks):
            starts.append(st.astype(BF16))
            st = st * decay[c][:, ks_] + ds_all[:, c * GLA_DK:(c + 1) * GLA_DK]
        st_ref[hh] = st
        o_ref[:, vs_] = _dot(a, vh) + _dot_nt(chunk_expand(qh), jnp.concatenate(starts, axis=1))

    g = _dot(hb, wqkvg_ref[:, O_G:O_F])
    lg = _dot(hbp, wlg_ref[...])
    gb = _dot(hbp, wgab_ref[:, D_MODEL:])
    ga = _dot(hb, wgab_ref[:, :D_MODEL])

    r_gate = _sigmoid(jnp.concatenate(r_parts, axis=1) + vec(V_LRU_BA))
    i_gate = _sigmoid(jnp.concatenate(i_parts, axis=1) + vec(V_LRU_BX))
    log_a = (-LRU_C) * r_gate * _softplus(vec(V_LRU_A_PARAM))
    a_s = jnp.exp(log_a)
    u_s = jnp.sqrt(-jnp.tanh(log_a) * (1.0 + a_s * a_s)) * (i_gate * xc)

    h_loc = u_s[0:SUBLANES, :]
    a_cum = a_s[0:SUBLANES, :]
    hl_ref[0:SUBLANES, :] = h_loc
    ac_ref[0:SUBLANES, :] = a_cum
    for r in range(1, seg_len):
        rs = slice(r * SUBLANES, (r + 1) * SUBLANES)
        h_loc = a_s[rs, :] * h_loc + u_s[rs, :]
        a_cum = a_s[rs, :] * a_cum
        hl_ref[rs, :] = h_loc
        ac_ref[rs, :] = a_cum
    sub = lax.broadcasted_iota(jnp.int32, (SUBLANES, LRU_PAD), 0)
    e_s, p_s = h_loc, a_cum
    s = 1
    while s < SUBLANES:
        m = sub >= s
        e_s = jnp.where(m, e_s + p_s * pltpu.roll(e_s, s, 0), e_s)
        p_s = jnp.where(m, p_s * pltpu.roll(p_s, s, 0), p_s)
        s *= 2
    h_carry = hc_ref[...]
    seg_end = e_s + p_s * h_carry
    seg_start = jnp.where(sub == 0, h_carry, pltpu.roll(seg_end, 1, 0))
    hc_ref[...] = seg_end[SUBLANES - 1:SUBLANES, :]
    for r in range(seg_len):
        rs = slice(r * SUBLANES, (r + 1) * SUBLANES)
        hl_ref[rs, :] = hl_ref[rs, :] + ac_ref[rs, :] * seg_start

    og = []
    for hh in range(GLA_HEADS):
        vs_ = slice(hh * GLA_DV, (hh + 1) * GLA_DV)
        gh = g[:, vs_]
        og.append(_rms_norm(o_ref[:, vs_], vec(V_GLA_ONORM, GLA_DV)) * (gh * _sigmoid(gh)))
    y_a = _dot(jnp.concatenate(og, axis=1).astype(BF16), gwo_ref[...])

    y_b = _dot((hl_ref[...] * _gelu_tanh(lg)).astype(BF16), lwo_ref[...])
    gated_a = _sigmoid(ga) * y_a
    _to_token_major(_sigmoid(gb) * y_b, slab_ref, mb_ref)

    merged = (gated_a + mb_ref[...]).astype(BF16)
    out_ref[...] = x + _dot(merged, wout_ref[...])


def _load_as_bf16(src_hbm, dst_ref, stage_ref, sem):
    rows = stage_ref.shape[1]
    n_chunks = src_hbm.shape[0] // rows

    def chunk_copy(c):
        return pltpu.make_async_copy(src_hbm.at[pl.ds(c * rows, rows), :], stage_ref.at[c % 2], sem.at[c % 2])

    chunk_copy(0).start()
    for c in range(n_chunks):
        if c + 1 < n_chunks:
            chunk_copy(c + 1).start()
        chunk_copy(c).wait()
        dst_ref[c * rows:(c + 1) * rows, :] = stage_ref[c % 2].astype(BF16)


def _ffn_kernel(x_ref, vec_ref, wup_hbm, wdown_hbm, out_ref,
                u_ref, uprev_ref, slab_ref, y_ref, wup_ref, wdown_ref, up_stage_ref, down_stage_ref, sem):
    T = x_ref.shape[0]

    @pl.when(pl.program_id(0) == 0)
    def _():
        uprev_ref[...] = jnp.zeros_like(uprev_ref)
        _load_as_bf16(wup_hbm, wup_ref, up_stage_ref, sem)
        _load_as_bf16(wdown_hbm, wdown_ref, down_stage_ref, sem)

    def vec(r, n=2 * D_FF):
        return vec_ref[r:r + 1, :n]

    n_sub = u_ref.shape[0]
    H = T // n_sub
    us = []
    for sb in range(n_sub):
        x = x_ref[sb * H:(sb + 1) * H, :]
        hbp = _segment_major(_rms_norm(x, vec(V_FFN_NORM, D_MODEL)).astype(BF16))
        u = _dot(hbp, wup_ref[...])
        u_ref[sb, FFN_HALO:FFN_HALO + H, :] = u
        _fill_halo(u_ref.at[sb], uprev_ref, u, FFN_HALO)
        us.append(u)
    for sb in range(n_sub):
        uc = vec(V_FFN_CONV_B) + us[sb] * vec(V_FFN_CONV_W + FFN_CONV - 1)
        for d in range(1, FFN_CONV):
            rows = slice(FFN_HALO - d * SUBLANES, FFN_HALO - d * SUBLANES + H)
            uc = uc + u_ref[sb, rows, :] * vec(V_FFN_CONV_W + FFN_CONV - 1 - d)
        act = (_gelu_tanh(uc[:, :D_FF]) * uc[:, D_FF:]).astype(BF16)
        _to_token_major(_dot(act, wdown_ref[...]), slab_ref.at[sb], y_ref, sb * H)
        rows = slice(sb * H, (sb + 1) * H)
        out_ref[rows, :] = _rms_norm(x_ref[rows, :] + y_ref[rows, :], vec(V_FINAL_NORM, D_MODEL))


def _const_spec(shape):
    nd = len(shape)
    return pl.BlockSpec(shape, lambda i, _nd=nd: (0,) * _nd, pipeline_mode=pl.Buffered(1))


def _pad_cols(w, n):
    return jnp.pad(w, ((0, 0), (0, n - w.shape[1])))


def _block_rows(w):
    rows = w.reshape(LRU_WIDTH, LRU_BW).astype(F32)
    return jnp.pad(rows, ((0, LRU_PAD - LRU_WIDTH), (0, MXU_DIM - LRU_BW)))


def _pack_rows(vectors, n):
    return jnp.concatenate([_pad_cols(v.reshape(-1, v.shape[-1]).astype(F32), n) for v in vectors], axis=0)


@jax.jit
def _forward(x, attn_norm, w_in, gla_wf2, gla_bf2, gla_onorm, gla_wo, lru_conv_w, lru_conv_b, lru_wa, lru_ba,
             lru_wx, lru_bx, lru_a_param, lru_wo, w_out, ffn_norm, ffn_wup, ffn_conv_w, ffn_conv_b, ffn_wdown,
             final_norm):
    B, S, D = x.shape
    assert B == 1 and D == D_MODEL and S % SEQ_BLOCK == 0 and attn_norm.shape[0] == 1
    T = SEQ_BLOCK
    x2 = x.reshape(S, D)

    wi = w_in[0]
    w_qkvg = wi[:, O_Q:O_F].astype(BF16)
    w_f = _pad_cols(wi[:, O_F:O_XL], RANK_PAD).astype(BF16)
    w_xl = wi[:, O_XL:O_XL + LRU_PAD].astype(BF16)
    w_lg = wi[:, O_LG:O_LG + LRU_PAD].astype(BF16)
    w_gab = wi[:, O_GA:O_END].astype(BF16)
    wf2 = jnp.pad(gla_wf2[0], ((0, RANK_PAD - GLA_GATE_RANK), (0, 0)))
    lwo = jnp.pad(lru_wo[0], ((0, LRU_PAD - LRU_WIDTH), (0, 0)))

    mixer_vec = _pack_rows([attn_norm[0], gla_bf2[0], gla_onorm[0], lru_conv_b[0], lru_ba[0], lru_bx[0],
                            lru_a_param[0], lru_conv_w[0]], LRU_PAD)
    mixer_in = [
        x2, mixer_vec, w_qkvg, w_f, w_xl, w_lg, w_gab, wf2.astype(BF16), gla_wo[0].astype(BF16),
        _block_rows(lru_wa[0]), _block_rows(lru_wx[0]), lwo.astype(BF16), w_out[0].astype(BF16),
    ]
    blk = pl.BlockSpec((T, D), lambda i: (i, 0))
    params = pltpu.CompilerParams(dimension_semantics=("arbitrary",), vmem_limit_bytes=VMEM_LIMIT_BYTES)
    x1 = pl.pallas_call(
        _mixer_kernel,
        grid=(S // T,),
        in_specs=[blk] + [_const_spec(a.shape) for a in mixer_in[1:]],
        out_specs=blk,
        out_shape=jax.ShapeDtypeStruct((S, D), F32),
        scratch_shapes=[
            pltpu.VMEM((GLA_HEADS, GLA_DV, GLA_DK), F32),
            pltpu.VMEM((LRU_HALO + T, LRU_PAD), F32),
            pltpu.VMEM((LRU_HALO, LRU_PAD), F32),
            pltpu.VMEM((1, LRU_PAD), F32),
            pltpu.VMEM((T, GLA_VAL), F32),
            pltpu.VMEM((T, LRU_PAD), F32),
            pltpu.VMEM((T, LRU_PAD), F32),
            pltpu.VMEM((D // LANES, T, LANES), F32),
            pltpu.VMEM((T, D), F32),
            pltpu.VMEM((BAND_ROWS, 2 * LANES), BF16),
        ],
        compiler_params=params,
        name="mixer",
    )(*mixer_in)

    ffn_vec = _pack_rows([ffn_norm[0], final_norm, ffn_conv_b[0], ffn_conv_w[0]], 2 * D_FF)
    ffn_in = [x1, ffn_vec, ffn_wup[0], ffn_wdown[0]]
    assert S % (FFN_SUB * T) == 0
    ffn_blk = pl.BlockSpec((FFN_SUB * T, D), lambda i: (i, 0))
    hbm = pl.BlockSpec(memory_space=pl.ANY)
    out = pl.pallas_call(
        _ffn_kernel,
        grid=(S // (FFN_SUB * T),),
        in_specs=[ffn_blk, _const_spec(ffn_vec.shape), hbm, hbm],
        out_specs=ffn_blk,
        out_shape=jax.ShapeDtypeStruct((S, D), F32),
        scratch_shapes=[
            pltpu.VMEM((FFN_SUB, FFN_HALO + T, 2 * D_FF), F32),
            pltpu.VMEM((FFN_HALO, 2 * D_FF), F32),
            pltpu.VMEM((FFN_SUB, D // LANES, T, LANES), F32),
            pltpu.VMEM((FFN_SUB * T, D), F32),
            pltpu.VMEM((D, 2 * D_FF), BF16),
            pltpu.VMEM((D_FF, D), BF16),
            pltpu.VMEM((2, D // WEIGHT_CHUNKS, 2 * D_FF), F32),
            pltpu.VMEM((2, D_FF // WEIGHT_CHUNKS, D), F32),
            pltpu.SemaphoreType.DMA((2,)),
        ],
        compiler_params=params,
        name="ffn",
    )(*ffn_in)
    return out.reshape(B, S, D)


def kernel(x, attn_norm, w_in, gla_wf2, gla_bf2, gla_onorm, gla_wo, lru_conv_w, lru_conv_b, lru_wa, lru_ba, lru_wx, lru_bx, lru_a_param, lru_wo, w_out, ffn_norm, ffn_wup, ffn_conv_w, ffn_conv_b, ffn_wdown, final_norm):
    return _forward(x, attn_norm, w_in, gla_wf2, gla_bf2, gla_onorm, gla_wo, lru_conv_w, lru_conv_b, lru_wa, lru_ba,
                    lru_wx, lru_bx, lru_a_param, lru_wo, w_out, ffn_norm, ffn_wup, ffn_conv_w, ffn_conv_b,
                    ffn_wdown, final_norm)
```

```python
import jax
import jax.numpy as jnp
import numpy as np
from jax import lax
from jax.experimental import pallas as pl
from jax.experimental.pallas import tpu as pltpu

D_MODEL = 1024
GLA_HEADS = 4
GLA_DK = 128
GLA_DV = 256
GLA_KEY = GLA_HEADS * GLA_DK
GLA_VAL = GLA_HEADS * GLA_DV
GLA_GATE_RANK = 16
GLA_GATE_NORM = 16.0
GLA_CHUNK = 64
LRU_WIDTH = 1344
LRU_BLOCKS = 8
LRU_BW = LRU_WIDTH // LRU_BLOCKS
LRU_CONV = 4
LRU_C = 8.0
D_FF = 3 * D_MODEL
FFN_CONV = 3
EPS = 1e-6

LANES = 128
SUBLANES = 8
MXU_DIM = 256
V7X_VMEM_BYTES = 64 * 1024 * 1024
VMEM_LIMIT_BYTES = V7X_VMEM_BYTES - 6 * 1024 * 1024

LRU_PAD = ((LRU_WIDTH + LANES - 1) // LANES) * LANES
RANK_PAD = LANES
SEQ_BLOCK = 256
LRU_HALO = (LRU_CONV - 1) * SUBLANES
FFN_HALO = (FFN_CONV - 1) * SUBLANES
FFN_SUB = 2
WEIGHT_CHUNKS = 8

(O_Q, O_K, O_V, O_G, O_F, O_XL, O_LG, O_GA, O_GB, O_END) = [int(v) for v in np.cumsum(
    [0, GLA_KEY, GLA_KEY, GLA_VAL, GLA_VAL, GLA_GATE_RANK, LRU_WIDTH, LRU_WIDTH, D_MODEL, D_MODEL])]

(V_ATTN_NORM, V_GLA_BF2, V_GLA_ONORM, V_LRU_CONV_B, V_LRU_BA, V_LRU_BX, V_LRU_A_PARAM, V_LRU_CONV_W) = range(8)
(V_FFN_NORM, V_FINAL_NORM, V_FFN_CONV_B, V_FFN_CONV_W) = range(4)

BF16 = jnp.bfloat16
F32 = jnp.float32


def _band_tiles():
    tiles = []
    for c0 in range(0, LRU_PAD, LANES):
        c1 = min(c0 + LANES, LRU_PAD)
        last_col = min(c1, LRU_WIDTH) - 1
        b_lo = c0 // LRU_BW
        b_hi = last_col // LRU_BW
        k0 = (b_lo * LRU_BW) // LANES * LANES
        k1 = min(-(-((b_hi + 1) * LRU_BW) // LANES) * LANES, LRU_PAD)
        tiles.append((k0, k1, c0, c1))
    return tuple(tiles)


BAND_TILES = _band_tiles()
BAND_ROWS = sum(k1 - k0 for k0, k1, _, _ in BAND_TILES)


def _dot(a, b):
    return jnp.dot(a, b, preferred_element_type=F32)


def _dot_nt(a, b):
    return lax.dot_general(a, b, (((1,), (1,)), ((), ())), preferred_element_type=F32)


def _dot_tn(a, b):
    return lax.dot_general(a, b, (((0,), (0,)), ((), ())), preferred_element_type=F32)


def _sigmoid(x):
    return 1.0 / (1.0 + jnp.exp(-x))


def _softplus(x):
    return jnp.maximum(x, 0.0) + jnp.log1p(jnp.exp(-jnp.abs(x)))


def _gelu_tanh(x):
    c = float(np.sqrt(2.0 / np.pi))
    hx = 0.5 * x
    return hx + hx * jnp.tanh(x * (c + (0.044715 * c) * (x * x)))


def _rms_norm(x, g):
    ms = jnp.mean(x * x, axis=-1, keepdims=True)
    return x * lax.rsqrt(ms + EPS) * g


def _segment_major(hb):
    T = hb.shape[0]
    seg_len = T // SUBLANES
    p = lax.broadcasted_iota(jnp.int32, (T, T), 0)
    t = lax.broadcasted_iota(jnp.int32, (T, T), 1)
    perm = jnp.where(t == (p % SUBLANES) * seg_len + p // SUBLANES, 1.0, 0.0).astype(BF16)
    return _dot(perm, hb).astype(BF16)


def _fill_halo(buf_ref, prev_ref, cur, halo):
    T = cur.shape[0]
    sub = lax.broadcasted_iota(jnp.int32, (SUBLANES, cur.shape[1]), 0)
    for off in range(0, halo, SUBLANES):
        rows = slice(off, off + SUBLANES)
        tail = cur[T - halo + off:T - halo + off + SUBLANES, :]
        buf_ref[rows, :] = pltpu.roll(jnp.where(sub == SUBLANES - 1, prev_ref[rows, :], tail), 1, 0)
    prev_ref[...] = cur[T - halo:T, :]


def _to_token_major(val, slab_ref, dst_ref, row0=0):
    T, d = val.shape
    seg_len = T // SUBLANES
    for n in range(d // LANES):
        slab_ref[n] = val[:, n * LANES:(n + 1) * LANES]
    for j in range(T // SUBLANES):
        t0 = j * SUBLANES
        start = (t0 % seg_len) * SUBLANES + t0 // seg_len
        for n in range(d // LANES):
            dst_ref[row0 + t0:row0 + t0 + SUBLANES, n * LANES:(n + 1) * LANES] = (
                slab_ref[n, pl.ds(start, SUBLANES, stride=SUBLANES), :])


def _build_band(w_refs, wband_ref):
    src = lax.broadcasted_iota(jnp.int32, (MXU_DIM, LANES), 0)
    dst = lax.broadcasted_iota(jnp.int32, (MXU_DIM, LANES), 1)
    row0 = 0
    for (k0, k1, c0, c1) in BAND_TILES:
        row_block = (lax.broadcasted_iota(jnp.int32, (k1 - k0, MXU_DIM), 0) + k0) // LRU_BW
        for gate, w_ref in enumerate(w_refs):
            rows = w_ref[k0:k1, :]
            tile = jnp.zeros((k1 - k0, LANES), F32)
            for n in range(c0 // LRU_BW, min((c1 - 1) // LRU_BW, LRU_BLOCKS - 1) + 1):
                place = jnp.where(src == dst + (c0 - n * LRU_BW), 1.0, 0.0).astype(BF16)
                tile = tile + _dot(jnp.where(row_block == n, rows, 0.0).astype(BF16), place)
            wband_ref[row0:row0 + (k1 - k0), gate * LANES:(gate + 1) * LANES] = tile.astype(BF16)
        row0 += k1 - k0


def _load_transposed_bf16(src_hbm, row0, dst_ref, stage_ref, sem):
    rows = stage_ref.shape[1]
    n_chunks = dst_ref.shape[1] // rows

    def chunk_copy(c):
        return pltpu.make_async_copy(src_hbm.at[pl.ds(row0 + c * rows, rows), :], stage_ref.at[c % 2], sem.at[c % 2])

    chunk_copy(0).start()
    for c in range(n_chunks):
        if c + 1 < n_chunks:
            chunk_copy(c + 1).start()
        chunk_copy(c).wait()
        dst_ref[:, c * rows:(c + 1) * rows] = stage_ref[c % 2].T.astype(BF16)


def _mixer_kernel(x_ref, vec_ref, wint_hbm, wf2_ref, gwo_ref, wa_ref, wx_ref, lwo_ref, wout_ref,
                  out_ref,
                  st_ref, xl_ref, xprev_ref, hc_ref, o_ref, hl_ref, ac_ref, slab_ref, mb_ref, wband_ref,
                  wqkvg_ref, wf_ref, wxl_ref, wlg_ref, wgab_ref, wstage_ref, sem):
    T = x_ref.shape[0]
    n_chunks = T // GLA_CHUNK
    seg_len = T // SUBLANES

    @pl.when(pl.program_id(0) == 0)
    def _():
        st_ref[...] = jnp.zeros_like(st_ref)
        xprev_ref[...] = jnp.zeros_like(xprev_ref)
        hc_ref[...] = jnp.zeros_like(hc_ref)
        _build_band((wa_ref, wx_ref), wband_ref)
        for dst_ref, row0 in ((wqkvg_ref, O_Q), (wf_ref, O_F), (wxl_ref, O_XL), (wlg_ref, O_LG), (wgab_ref, O_GA)):
            _load_transposed_bf16(wint_hbm, row0, dst_ref, wstage_ref, sem)

    def vec(r, n=LRU_PAD):
        return vec_ref[r:r + 1, :n]

    x = x_ref[...]
    hb = _rms_norm(x, vec(V_ATTN_NORM, D_MODEL)).astype(BF16)

    f_low = _dot(hb, wf_ref[...]).astype(BF16)
    z = _dot(f_low, wf2_ref[...]) + vec(V_GLA_BF2, GLA_KEY)
    hbp = _segment_major(hb)
    xl = _dot(hbp, wxl_ref[...])
    xl_ref[LRU_HALO:LRU_HALO + T, :] = xl
    _fill_halo(xl_ref, xprev_ref, xl, LRU_HALO)
    gk = (jnp.minimum(z, 0.0) - jnp.log1p(jnp.exp(-jnp.abs(z)))) * (1.0 / GLA_GATE_NORM)

    row = lax.broadcasted_iota(jnp.int32, (T, T), 0)
    col = lax.broadcasted_iota(jnp.int32, (T, T), 1)
    intra = (row >= col) & ((row // GLA_CHUNK) == (col // GLA_CHUNK))
    tri = jnp.where(intra, 1.0, 0.0).astype(BF16)
    g1 = gk.astype(BF16)
    r1 = gk - g1.astype(F32)
    g2 = r1.astype(BF16)
    g3 = (r1 - g2.astype(F32)).astype(BF16)
    b = _dot(tri, g1) + _dot(tri, g2) + _dot(tri, g3)
    q = _dot(hb, wqkvg_ref[:, O_Q:O_K]) * (GLA_DK ** -0.5)
    k = _dot(hb, wqkvg_ref[:, O_K:O_V])
    v = _dot(hb, wqkvg_ref[:, O_V:O_G]).astype(BF16)

    xc = vec(V_LRU_CONV_B) + xl * vec(V_LRU_CONV_W + LRU_CONV - 1)
    for d in range(1, LRU_CONV):
        rows = slice(LRU_HALO - d * SUBLANES, LRU_HALO - d * SUBLANES + T)
        xc = xc + xl_ref[rows, :] * vec(V_LRU_CONV_W + LRU_CONV - 1 - d)
    xcb = xc.astype(BF16)
    row0 = 0
    r_parts, i_parts = [], []
    for (k0, k1, c0, c1) in BAND_TILES:
        both = _dot(xcb[:, k0:k1], wband_ref[row0:row0 + (k1 - k0), :])
        r_parts.append(both[:, :LANES])
        i_parts.append(both[:, LANES:])
        row0 += k1 - k0

    b_last_rows = [b[(c + 1) * GLA_CHUNK - 1:(c + 1) * GLA_CHUNK, :] for c in range(n_chunks)]
    b_last = jnp.concatenate([jnp.broadcast_to(r, (GLA_CHUNK, GLA_KEY)) for r in b_last_rows], axis=0)
    decay = [jnp.exp(r) for r in b_last_rows]
    q_t = (q * jnp.exp(b)).astype(BF16)
    k_t = (k * jnp.exp(-b)).astype(BF16)
    k_s = (k * jnp.exp(b_last - b)).astype(BF16)

    def chunk_expand(m):
        cols = []
        for c in range(n_chunks):
            parts = []
            if c > 0:
                parts.append(jnp.zeros((c * GLA_CHUNK, GLA_DK), m.dtype))
            parts.append(m[c * GLA_CHUNK:(c + 1) * GLA_CHUNK, :])
            if c < n_chunks - 1:
                parts.append(jnp.zeros((T - (c + 1) * GLA_CHUNK, GLA_DK), m.dtype))
            cols.append(jnp.concatenate(parts, axis=0))
        return jnp.concatenate(cols, axis=1)

    for hh in range(GLA_HEADS):
        ks_ = slice(hh * GLA_DK, (hh + 1) * GLA_DK)
        vs_ = slice(hh * GLA_DV, (hh + 1) * GLA_DV)
        qh = q_t[:, ks_]
        vh = v[:, vs_]
        a = jnp.where(intra, _dot_nt(qh, k_t[:, ks_]), 0.0).astype(BF16)
        ds_all = _dot_tn(vh, chunk_expand(k_s[:, ks_]))
        st = st_ref[hh]
        starts = []
        for c in range(n_chunks):
            starts.append(st.astype(BF16))
            st = st * decay[c][:, ks_] + ds_all[:, c * GLA_DK:(c + 1) * GLA_DK]
        st_ref[hh] = st
        o_ref[:, vs_] = _dot(a, vh) + _dot_nt(chunk_expand(qh), jnp.concatenate(starts, axis=1))

    g = _dot(hb, wqkvg_ref[:, O_G:O_F])
    lg = _dot(hbp, wlg_ref[...])
    gb = _dot(hbp, wgab_ref[:, D_MODEL:])
    ga = _dot(hb, wgab_ref[:, :D_MODEL])

    r_gate = _sigmoid(jnp.concatenate(r_parts, axis=1) + vec(V_LRU_BA))
    i_gate = _sigmoid(jnp.concatenate(i_parts, axis=1) + vec(V_LRU_BX))
    log_a = (-LRU_C) * r_gate * _softplus(vec(V_LRU_A_PARAM))
    a_s = jnp.exp(log_a)
    u_s = jnp.sqrt(-jnp.tanh(log_a) * (1.0 + a_s * a_s)) * (i_gate * xc)

    h_loc = u_s[0:SUBLANES, :]
    a_cum = a_s[0:SUBLANES, :]
    hl_ref[0:SUBLANES, :] = h_loc
    ac_ref[0:SUBLANES, :] = a_cum
    for r in range(1, seg_len):
        rs = slice(r * SUBLANES, (r + 1) * SUBLANES)
        h_loc = a_s[rs, :] * h_loc + u_s[rs, :]
        a_cum = a_s[rs, :] * a_cum
        hl_ref[rs, :] = h_loc
        ac_ref[rs, :] = a_cum
    sub = lax.broadcasted_iota(jnp.int32, (SUBLANES, LRU_PAD), 0)
    e_s, p_s = h_loc, a_cum
    s = 1
    while s < SUBLANES:
        m = sub >= s
        e_s = jnp.where(m, e_s + p_s * pltpu.roll(e_s, s, 0), e_s)
        p_s = jnp.where(m, p_s * pltpu.roll(p_s, s, 0), p_s)
        s *= 2
    h_carry = hc_ref[...]
    seg_end = e_s + p_s * h_carry
    seg_start = jnp.where(sub == 0, h_carry, pltpu.roll(seg_end, 1, 0))
    hc_ref[...] = seg_end[SUBLANES - 1:SUBLANES, :]
    for r in range(seg_len):
        rs = slice(r * SUBLANES, (r + 1) * SUBLANES)
        hl_ref[rs, :] = hl_ref[rs, :] + ac_ref[rs, :] * seg_start

    og = []
    for hh in range(GLA_HEADS):
        vs_ = slice(hh * GLA_DV, (hh + 1) * GLA_DV)
        gh = g[:, vs_]
        og.append(_rms_norm(o_ref[:, vs_], vec(V_GLA_ONORM, GLA_DV)) * (gh * _sigmoid(gh)))
    y_a = _dot(jnp.concatenate(og, axis=1).astype(BF16), gwo_ref[...])

    y_b = _dot((hl_ref[...] * _gelu_tanh(lg)).astype(BF16), lwo_ref[...])
    gated_a = _sigmoid(ga) * y_a
    _to_token_major(_sigmoid(gb) * y_b, slab_ref, mb_ref)

    merged = (gated_a + mb_ref[...]).astype(BF16)
    out_ref[...] = x + _dot(merged, wout_ref[...])


def _load_as_bf16(src_hbm, dst_ref, stage_ref, sem):
    rows = stage_ref.shape[1]
    n_chunks = src_hbm.shape[0] // rows

    def chunk_copy(c):
        return pltpu.make_async_copy(src_hbm.at[pl.ds(c * rows, rows), :], stage_ref.at[c % 2], sem.at[c % 2])

    chunk_copy(0).start()
    for c in range(n_chunks):
        if c + 1 < n_chunks:
            chunk_copy(c + 1).start()
        chunk_copy(c).wait()
        dst_ref[c * rows:(c + 1) * rows, :] = stage_ref[c % 2].astype(BF16)


def _ffn_kernel(x_ref, vec_ref, wup_hbm, wdown_hbm, out_ref,
                u_ref, uprev_ref, slab_ref, y_ref, wup_ref, wdown_ref, up_stage_ref, down_stage_ref, sem):
    T = x_ref.shape[0]

    @pl.when(pl.program_id(0) == 0)
    def _():
        uprev_ref[...] = jnp.zeros_like(uprev_ref)
        _load_as_bf16(wup_hbm, wup_ref, up_stage_ref, sem)
        _load_as_bf16(wdown_hbm, wdown_ref, down_stage_ref, sem)

    def vec(r, n=2 * D_FF):
        return vec_ref[r:r + 1, :n]

    n_sub = u_ref.shape[0]
    H = T // n_sub
    us = []
    for sb in range(n_sub):
        x = x_ref[sb * H:(sb + 1) * H, :]
        hbp = _segment_major(_rms_norm(x, vec(V_FFN_NORM, D_MODEL)).astype(BF16))
        u = _dot(hbp, wup_ref[...])
        u_ref[sb, FFN_HALO:FFN_HALO + H, :] = u
        _fill_halo(u_ref.at[sb], uprev_ref, u, FFN_HALO)
        us.append(u)
    for sb in range(n_sub):
        uc = vec(V_FFN_CONV_B) + us[sb] * vec(V_FFN_CONV_W + FFN_CONV - 1)
        for d in range(1, FFN_CONV):
            rows = slice(FFN_HALO - d * SUBLANES, FFN_HALO - d * SUBLANES + H)
            uc = uc + u_ref[sb, rows, :] * vec(V_FFN_CONV_W + FFN_CONV - 1 - d)
        act = (_gelu_tanh(uc[:, :D_FF]) * uc[:, D_FF:]).astype(BF16)
        _to_token_major(_dot(act, wdown_ref[...]), slab_ref.at[sb], y_ref, sb * H)
        rows = slice(sb * H, (sb + 1) * H)
        out_ref[rows, :] = _rms_norm(x_ref[rows, :] + y_ref[rows, :], vec(V_FINAL_NORM, D_MODEL))


def _const_spec(shape):
    nd = len(shape)
    return pl.BlockSpec(shape, lambda i, _nd=nd: (0,) * _nd, pipeline_mode=pl.Buffered(1))


def _pad_cols(w, n):
    return jnp.pad(w, ((0, 0), (0, n - w.shape[1])))


def _block_rows(w):
    rows = w.reshape(LRU_WIDTH, LRU_BW).astype(F32)
    return jnp.pad(rows, ((0, LRU_PAD - LRU_WIDTH), (0, MXU_DIM - LRU_BW)))


def _pack_rows(vectors, n):
    return jnp.concatenate([_pad_cols(v.reshape(-1, v.shape[-1]).astype(F32), n) for v in vectors], axis=0)


@jax.jit
def _forward(x, attn_norm, w_in, gla_wf2, gla_bf2, gla_onorm, gla_wo, lru_conv_w, lru_conv_b, lru_wa, lru_ba,
             lru_wx, lru_bx, lru_a_param, lru_wo, w_out, ffn_norm, ffn_wup, ffn_conv_w, ffn_conv_b, ffn_wdown,
             final_norm):
    B, S, D = x.shape
    assert B == 1 and D == D_MODEL and S % SEQ_BLOCK == 0 and attn_norm.shape[0] == 1
    T = SEQ_BLOCK
    x2 = x.reshape(S, D)

    w_in_t = w_in[0].T
    wf2 = jnp.pad(gla_wf2[0], ((0, RANK_PAD - GLA_GATE_RANK), (0, 0)))
    lwo = jnp.pad(lru_wo[0], ((0, LRU_PAD - LRU_WIDTH), (0, 0)))

    mixer_vec = _pack_rows([attn_norm[0], gla_bf2[0], gla_onorm[0], lru_conv_b[0], lru_ba[0], lru_bx[0],
                            lru_a_param[0], lru_conv_w[0]], LRU_PAD)
    mixer_in = [
        x2, mixer_vec, w_in_t, wf2.astype(BF16), gla_wo[0].astype(BF16),
        _block_rows(lru_wa[0]), _block_rows(lru_wx[0]), lwo.astype(BF16), w_out[0].astype(BF16),
    ]
    blk = pl.BlockSpec((T, D), lambda i: (i, 0))
    hbm = pl.BlockSpec(memory_space=pl.ANY)
    params = pltpu.CompilerParams(dimension_semantics=("arbitrary",), vmem_limit_bytes=VMEM_LIMIT_BYTES)
    x1 = pl.pallas_call(
        _mixer_kernel,
        grid=(S // T,),
        in_specs=[blk, _const_spec(mixer_vec.shape), hbm] + [_const_spec(a.shape) for a in mixer_in[3:]],
        out_specs=blk,
        out_shape=jax.ShapeDtypeStruct((S, D), F32),
        scratch_shapes=[
            pltpu.VMEM((GLA_HEADS, GLA_DV, GLA_DK), F32),
            pltpu.VMEM((LRU_HALO + T, LRU_PAD), F32),
            pltpu.VMEM((LRU_HALO, LRU_PAD), F32),
            pltpu.VMEM((1, LRU_PAD), F32),
            pltpu.VMEM((T, GLA_VAL), F32),
            pltpu.VMEM((T, LRU_PAD), F32),
            pltpu.VMEM((T, LRU_PAD), F32),
            pltpu.VMEM((D // LANES, T, LANES), F32),
            pltpu.VMEM((T, D), F32),
            pltpu.VMEM((BAND_ROWS, 2 * LANES), BF16),
            pltpu.VMEM((D, O_F - O_Q), BF16),
            pltpu.VMEM((D, RANK_PAD), BF16),
            pltpu.VMEM((D, LRU_PAD), BF16),
            pltpu.VMEM((D, LRU_PAD), BF16),
            pltpu.VMEM((D, O_END - O_GA), BF16),
            pltpu.VMEM((2, LANES, D), F32),
            pltpu.SemaphoreType.DMA((2,)),
        ],
        compiler_params=params,
        name="mixer",
    )(*mixer_in)

    ffn_vec = _pack_rows([ffn_norm[0], final_norm, ffn_conv_b[0], ffn_conv_w[0]], 2 * D_FF)
    ffn_in = [x1, ffn_vec, ffn_wup[0], ffn_wdown[0]]
    assert S % (FFN_SUB * T) == 0
    ffn_blk = pl.BlockSpec((FFN_SUB * T, D), lambda i: (i, 0))
    hbm = pl.BlockSpec(memory_space=pl.ANY)
    out = pl.pallas_call(
        _ffn_kernel,
        grid=(S // (FFN_SUB * T),),
        in_specs=[ffn_blk, _const_spec(ffn_vec.shape), hbm, hbm],
        out_specs=ffn_blk,
        out_shape=jax.ShapeDtypeStruct((S, D), F32),
        scratch_shapes=[
            pltpu.VMEM((FFN_SUB, FFN_HALO + T, 2 * D_FF), F32),
            pltpu.VMEM((FFN_HALO, 2 * D_FF), F32),
            pltpu.VMEM((FFN_SUB, D // LANES, T, LANES), F32),
            pltpu.VMEM((FFN_SUB * T, D), F32),
            pltpu.VMEM((D, 2 * D_FF), BF16),
            pltpu.VMEM((D_FF, D), BF16),
            pltpu.VMEM((2, D // WEIGHT_CHUNKS, 2 * D_FF), F32),
            pltpu.VMEM((2, D_FF // WEIGHT_CHUNKS, D), F32),
            pltpu.SemaphoreType.DMA((2,)),
        ],
        compiler_params=params,
        name="ffn",
    )(*ffn_in)
    return out.reshape(B, S, D)


def kernel(x, attn_norm, w_in, gla_wf2, gla_bf2, gla_onorm, gla_wo, lru_conv_w, lru_conv_b, lru_wa, lru_ba, lru_wx, lru_bx, lru_a_param, lru_wo, w_out, ffn_norm, ffn_wup, ffn_conv_w, ffn_conv_b, ffn_wdown, final_norm):
    return _forward(x, attn_norm, w_in, gla_wf2, gla_bf2, gla_onorm, gla_wo, lru_conv_w, lru_conv_b, lru_wa, lru_ba,
                    lru_wx, lru_bx, lru_a_param, lru_wo, w_out, ffn_norm, ffn_wup, ffn_conv_w, ffn_conv_b,
                    ffn_wdown, final_norm)
```

```python
import jax
import jax.numpy as jnp
import numpy as np
from jax import lax
from jax.experimental import pallas as pl
from jax.experimental.pallas import tpu as pltpu

D_MODEL = 1024
GLA_HEADS = 4
GLA_DK = 128
GLA_DV = 256
GLA_KEY = GLA_HEADS * GLA_DK
GLA_VAL = GLA_HEADS * GLA_DV
GLA_GATE_RANK = 16
GLA_GATE_NORM = 16.0
GLA_CHUNK = 64
LRU_WIDTH = 1344
LRU_BLOCKS = 8
LRU_BW = LRU_WIDTH // LRU_BLOCKS
LRU_CONV = 4
LRU_C = 8.0
D_FF = 3 * D_MODEL
FFN_CONV = 3
EPS = 1e-6

LANES = 128
SUBLANES = 8
MXU_DIM = 256
V7X_VMEM_BYTES = 64 * 1024 * 1024
VMEM_LIMIT_BYTES = V7X_VMEM_BYTES - 6 * 1024 * 1024

LRU_PAD = ((LRU_WIDTH + LANES - 1) // LANES) * LANES
RANK_PAD = LANES
SEQ_BLOCK = 256
LRU_HALO = (LRU_CONV - 1) * SUBLANES
FFN_HALO = (FFN_CONV - 1) * SUBLANES
FFN_SUB = 2
WEIGHT_CHUNKS = 8
W_IN_CHUNK_ROWS = 4 * LANES

(O_Q, O_K, O_V, O_G, O_F, O_XL, O_LG, O_GA, O_GB, O_END) = [int(v) for v in np.cumsum(
    [0, GLA_KEY, GLA_KEY, GLA_VAL, GLA_VAL, GLA_GATE_RANK, LRU_WIDTH, LRU_WIDTH, D_MODEL, D_MODEL])]

(V_ATTN_NORM, V_GLA_BF2, V_GLA_ONORM, V_LRU_CONV_B, V_LRU_BA, V_LRU_BX, V_LRU_A_PARAM, V_LRU_CONV_W) = range(8)
(V_FFN_NORM, V_FINAL_NORM, V_FFN_CONV_B, V_FFN_CONV_W) = range(4)

BF16 = jnp.bfloat16
F32 = jnp.float32


def _band_tiles():
    tiles = []
    for c0 in range(0, LRU_PAD, LANES):
        c1 = min(c0 + LANES, LRU_PAD)
        last_col = min(c1, LRU_WIDTH) - 1
        b_lo = c0 // LRU_BW
        b_hi = last_col // LRU_BW
        k0 = (b_lo * LRU_BW) // LANES * LANES
        k1 = min(-(-((b_hi + 1) * LRU_BW) // LANES) * LANES, LRU_PAD)
        tiles.append((k0, k1, c0, c1))
    return tuple(tiles)


BAND_TILES = _band_tiles()
BAND_ROWS = sum(k1 - k0 for k0, k1, _, _ in BAND_TILES)


def _dot(a, b):
    return jnp.dot(a, b, preferred_element_type=F32)


def _dot_nt(a, b):
    return lax.dot_general(a, b, (((1,), (1,)), ((), ())), preferred_element_type=F32)


def _dot_tn(a, b):
    return lax.dot_general(a, b, (((0,), (0,)), ((), ())), preferred_element_type=F32)


def _sigmoid(x):
    return 1.0 / (1.0 + jnp.exp(-x))


def _softplus(x):
    return jnp.maximum(x, 0.0) + jnp.log1p(jnp.exp(-jnp.abs(x)))


def _gelu_tanh(x):
    c = float(np.sqrt(2.0 / np.pi))
    hx = 0.5 * x
    return hx + hx * jnp.tanh(x * (c + (0.044715 * c) * (x * x)))


def _rms_norm(x, g):
    ms = jnp.mean(x * x, axis=-1, keepdims=True)
    return x * lax.rsqrt(ms + EPS) * g


def _segment_major(hb):
    T = hb.shape[0]
    seg_len = T // SUBLANES
    p = lax.broadcasted_iota(jnp.int32, (T, T), 0)
    t = lax.broadcasted_iota(jnp.int32, (T, T), 1)
    perm = jnp.where(t == (p % SUBLANES) * seg_len + p // SUBLANES, 1.0, 0.0).astype(BF16)
    return _dot(perm, hb).astype(BF16)


def _fill_halo(buf_ref, prev_ref, cur, halo):
    T = cur.shape[0]
    sub = lax.broadcasted_iota(jnp.int32, (SUBLANES, cur.shape[1]), 0)
    for off in range(0, halo, SUBLANES):
        rows = slice(off, off + SUBLANES)
        tail = cur[T - halo + off:T - halo + off + SUBLANES, :]
        buf_ref[rows, :] = pltpu.roll(jnp.where(sub == SUBLANES - 1, prev_ref[rows, :], tail), 1, 0)
    prev_ref[...] = cur[T - halo:T, :]


def _to_token_major(val, slab_ref, dst_ref, row0=0):
    T, d = val.shape
    seg_len = T // SUBLANES
    for n in range(d // LANES):
        slab_ref[n] = val[:, n * LANES:(n + 1) * LANES]
    for j in range(T // SUBLANES):
        t0 = j * SUBLANES
        start = (t0 % seg_len) * SUBLANES + t0 // seg_len
        for n in range(d // LANES):
            dst_ref[row0 + t0:row0 + t0 + SUBLANES, n * LANES:(n + 1) * LANES] = (
                slab_ref[n, pl.ds(start, SUBLANES, stride=SUBLANES), :])


def _build_band(w_refs, wband_ref):
    src = lax.broadcasted_iota(jnp.int32, (MXU_DIM, LANES), 0)
    dst = lax.broadcasted_iota(jnp.int32, (MXU_DIM, LANES), 1)
    row0 = 0
    for (k0, k1, c0, c1) in BAND_TILES:
        row_block = (lax.broadcasted_iota(jnp.int32, (k1 - k0, MXU_DIM), 0) + k0) // LRU_BW
        for gate, w_ref in enumerate(w_refs):
            rows = w_ref[k0:k1, :]
            tile = jnp.zeros((k1 - k0, LANES), F32)
            for n in range(c0 // LRU_BW, min((c1 - 1) // LRU_BW, LRU_BLOCKS - 1) + 1):
                place = jnp.where(src == dst + (c0 - n * LRU_BW), 1.0, 0.0).astype(BF16)
                tile = tile + _dot(jnp.where(row_block == n, rows, 0.0).astype(BF16), place)
            wband_ref[row0:row0 + (k1 - k0), gate * LANES:(gate + 1) * LANES] = tile.astype(BF16)
        row0 += k1 - k0


def _load_transposed_bf16(src_hbm, windows, stage_ref, sem):
    max_rows = stage_ref.shape[1]
    chunks = []
    for row0, dst_ref in windows:
        done = 0
        while done < dst_ref.shape[1]:
            rows = min(max_rows, dst_ref.shape[1] - done)
            chunks.append((row0 + done, rows, dst_ref, done))
            done += rows

    def chunk_copy(i):
        src_row, rows, _, _ = chunks[i]
        return pltpu.make_async_copy(src_hbm.at[pl.ds(src_row, rows), :], stage_ref.at[i % 2, pl.ds(0, rows), :],
                                     sem.at[i % 2])

    chunk_copy(0).start()
    for i, (_, rows, dst_ref, col) in enumerate(chunks):
        if i + 1 < len(chunks):
            chunk_copy(i + 1).start()
        chunk_copy(i).wait()
        dst_ref[:, col:col + rows] = stage_ref[i % 2, 0:rows, :].T.astype(BF16)


def _mixer_kernel(x_ref, vec_ref, wint_hbm, wf2_ref, gwo_ref, wa_ref, wx_ref, lwo_ref, wout_ref,
                  out_ref,
                  st_ref, xl_ref, xprev_ref, hc_ref, o_ref, hl_ref, ac_ref, slab_ref, mb_ref, wband_ref,
                  wqkvg_ref, wf_ref, wxl_ref, wlg_ref, wgab_ref, wstage_ref, sem):
    T = x_ref.shape[0]
    n_chunks = T // GLA_CHUNK
    seg_len = T // SUBLANES

    @pl.when(pl.program_id(0) == 0)
    def _():
        st_ref[...] = jnp.zeros_like(st_ref)
        xprev_ref[...] = jnp.zeros_like(xprev_ref)
        hc_ref[...] = jnp.zeros_like(hc_ref)
        _build_band((wa_ref, wx_ref), wband_ref)
        windows = ((O_Q, wqkvg_ref), (O_F, wf_ref), (O_XL, wxl_ref), (O_LG, wlg_ref), (O_GA, wgab_ref))
        _load_transposed_bf16(wint_hbm, windows, wstage_ref, sem)

    def vec(r, n=LRU_PAD):
        return vec_ref[r:r + 1, :n]

    x = x_ref[...]
    hb = _rms_norm(x, vec(V_ATTN_NORM, D_MODEL)).astype(BF16)

    f_low = _dot(hb, wf_ref[...]).astype(BF16)
    z = _dot(f_low, wf2_ref[...]) + vec(V_GLA_BF2, GLA_KEY)
    hbp = _segment_major(hb)
    xl = _dot(hbp, wxl_ref[...])
    xl_ref[LRU_HALO:LRU_HALO + T, :] = xl
    _fill_halo(xl_ref, xprev_ref, xl, LRU_HALO)
    gk = (jnp.minimum(z, 0.0) - jnp.log1p(jnp.exp(-jnp.abs(z)))) * (1.0 / GLA_GATE_NORM)

    row = lax.broadcasted_iota(jnp.int32, (T, T), 0)
    col = lax.broadcasted_iota(jnp.int32, (T, T), 1)
    intra = (row >= col) & ((row // GLA_CHUNK) == (col // GLA_CHUNK))
    tri = jnp.where(intra, 1.0, 0.0).astype(BF16)
    g1 = gk.astype(BF16)
    r1 = gk - g1.astype(F32)
    g2 = r1.astype(BF16)
    g3 = (r1 - g2.astype(F32)).astype(BF16)
    b = _dot(tri, g1) + _dot(tri, g2) + _dot(tri, g3)
    q = _dot(hb, wqkvg_ref[:, O_Q:O_K]) * (GLA_DK ** -0.5)
    k = _dot(hb, wqkvg_ref[:, O_K:O_V])
    v = _dot(hb, wqkvg_ref[:, O_V:O_G]).astype(BF16)

    xc = vec(V_LRU_CONV_B) + xl * vec(V_LRU_CONV_W + LRU_CONV - 1)
    for d in range(1, LRU_CONV):
        rows = slice(LRU_HALO - d * SUBLANES, LRU_HALO - d * SUBLANES + T)
        xc = xc + xl_ref[rows, :] * vec(V_LRU_CONV_W + LRU_CONV - 1 - d)
    xcb = xc.astype(BF16)
    row0 = 0
    r_parts, i_parts = [], []
    for (k0, k1, c0, c1) in BAND_TILES:
        both = _dot(xcb[:, k0:k1], wband_ref[row0:row0 + (k1 - k0), :])
        r_parts.append(both[:, :LANES])
        i_parts.append(both[:, LANES:])
        row0 += k1 - k0

    b_last_rows = [b[(c + 1) * GLA_CHUNK - 1:(c + 1) * GLA_CHUNK, :] for c in range(n_chunks)]
    b_last = jnp.concatenate([jnp.broadcast_to(r, (GLA_CHUNK, GLA_KEY)) for r in b_last_rows], axis=0)
    decay = [jnp.exp(r) for r in b_last_rows]
    q_t = (q * jnp.exp(b)).astype(BF16)
    k_t = (k * jnp.exp(-b)).astype(BF16)
    k_s = (k * jnp.exp(b_last - b)).astype(BF16)

    def chunk_expand(m):
        cols = []
        for c in range(n_chunks):
            parts = []
            if c > 0:
                parts.append(jnp.zeros((c * GLA_CHUNK, GLA_DK), m.dtype))
            parts.append(m[c * GLA_CHUNK:(c + 1) * GLA_CHUNK, :])
            if c < n_chunks - 1:
                parts.append(jnp.zeros((T - (c + 1) * GLA_CHUNK, GLA_DK), m.dtype))
            cols.append(jnp.concatenate(parts, axis=0))
        return jnp.concatenate(cols, axis=1)

    for hh in range(GLA_HEADS):
        ks_ = slice(hh * GLA_DK, (hh + 1) * GLA_DK)
        vs_ = slice(hh * GLA_DV, (hh + 1) * GLA_DV)
        qh = q_t[:, ks_]
        vh = v[:, vs_]
        a = jnp.where(intra, _dot_nt(qh, k_t[:, ks_]), 0.0).astype(BF16)
        ds_all = _dot_tn(vh, chunk_expand(k_s[:, ks_]))
        st = st_ref[hh]
        starts = []
        for c in range(n_chunks):
            starts.append(st.astype(BF16))
            st = st * decay[c][:, ks_] + ds_all[:, c * GLA_DK:(c + 1) * GLA_DK]
        st_ref[hh] = st
        o_ref[:, vs_] = _dot(a, vh) + _dot_nt(chunk_expand(qh), jnp.concatenate(starts, axis=1))

    g = _dot(hb, wqkvg_ref[:, O_G:O_F])
    lg = _dot(hbp, wlg_ref[...])
    gb = _dot(hbp, wgab_ref[:, D_MODEL:])
    ga = _dot(hb, wgab_ref[:, :D_MODEL])

    r_gate = _sigmoid(jnp.concatenate(r_parts, axis=1) + vec(V_LRU_BA))
    i_gate = _sigmoid(jnp.concatenate(i_parts, axis=1) + vec(V_LRU_BX))
    log_a = (-LRU_C) * r_gate * _softplus(vec(V_LRU_A_PARAM))
    a_s = jnp.exp(log_a)
    u_s = jnp.sqrt(-jnp.tanh(log_a) * (1.0 + a_s * a_s)) * (i_gate * xc)

    h_loc = u_s[0:SUBLANES, :]
    a_cum = a_s[0:SUBLANES, :]
    hl_ref[0:SUBLANES, :] = h_loc
    ac_ref[0:SUBLANES, :] = a_cum
    for r in range(1, seg_len):
        rs = slice(r * SUBLANES, (r + 1) * SUBLANES)
        h_loc = a_s[rs, :] * h_loc + u_s[rs, :]
        a_cum = a_s[rs, :] * a_cum
        hl_ref[rs, :] = h_loc
        ac_ref[rs, :] = a_cum
    sub = lax.broadcasted_iota(jnp.int32, (SUBLANES, LRU_PAD), 0)
    e_s, p_s = h_loc, a_cum
    s = 1
    while s < SUBLANES:
        m = sub >= s
        e_s = jnp.where(m, e_s + p_s * pltpu.roll(e_s, s, 0), e_s)
        p_s = jnp.where(m, p_s * pltpu.roll(p_s, s, 0), p_s)
        s *= 2
    h_carry = hc_ref[...]
    seg_end = e_s + p_s * h_carry
    seg_start = jnp.where(sub == 0, h_carry, pltpu.roll(seg_end, 1, 0))
    hc_ref[...] = seg_end[SUBLANES - 1:SUBLANES, :]
    for r in range(seg_len):
        rs = slice(r * SUBLANES, (r + 1) * SUBLANES)
        hl_ref[rs, :] = hl_ref[rs, :] + ac_ref[rs, :] * seg_start

    og = []
    for hh in range(GLA_HEADS):
        vs_ = slice(hh * GLA_DV, (hh + 1) * GLA_DV)
        gh = g[:, vs_]
        og.append(_rms_norm(o_ref[:, vs_], vec(V_GLA_ONORM, GLA_DV)) * (gh * _sigmoid(gh)))
    y_a = _dot(jnp.concatenate(og, axis=1).astype(BF16), gwo_ref[...])

    y_b = _dot((hl_ref[...] * _gelu_tanh(lg)).astype(BF16), lwo_ref[...])
    gated_a = _sigmoid(ga) * y_a
    _to_token_major(_sigmoid(gb) * y_b, slab_ref, mb_ref)

    merged = (gated_a + mb_ref[...]).astype(BF16)
    out_ref[...] = x + _dot(merged, wout_ref[...])


def _load_as_bf16(src_hbm, dst_ref, stage_ref, sem):
    rows = stage_ref.shape[1]
    n_chunks = src_hbm.shape[0] // rows

    def chunk_copy(c):
        return pltpu.make_async_copy(src_hbm.at[pl.ds(c * rows, rows), :], stage_ref.at[c % 2], sem.at[c % 2])

    chunk_copy(0).start()
    for c in range(n_chunks):
        if c + 1 < n_chunks:
            chunk_copy(c + 1).start()
        chunk_copy(c).wait()
        dst_ref[c * rows:(c + 1) * rows, :] = stage_ref[c % 2].astype(BF16)


def _ffn_kernel(x_ref, vec_ref, wup_hbm, wdown_hbm, out_ref,
                u_ref, uprev_ref, slab_ref, y_ref, wup_ref, wdown_ref, up_stage_ref, down_stage_ref, sem):
    T = x_ref.shape[0]

    @pl.when(pl.program_id(0) == 0)
    def _():
        uprev_ref[...] = jnp.zeros_like(uprev_ref)
        _load_as_bf16(wup_hbm, wup_ref, up_stage_ref, sem)
        _load_as_bf16(wdown_hbm, wdown_ref, down_stage_ref, sem)

    def vec(r, n=2 * D_FF):
        return vec_ref[r:r + 1, :n]

    n_sub = u_ref.shape[0]
    H = T // n_sub
    us = []
    for sb in range(n_sub):
        x = x_ref[sb * H:(sb + 1) * H, :]
        hbp = _segment_major(_rms_norm(x, vec(V_FFN_NORM, D_MODEL)).astype(BF16))
        u = _dot(hbp, wup_ref[...])
        u_ref[sb, FFN_HALO:FFN_HALO + H, :] = u
        _fill_halo(u_ref.at[sb], uprev_ref, u, FFN_HALO)
        us.append(u)
    for sb in range(n_sub):
        uc = vec(V_FFN_CONV_B) + us[sb] * vec(V_FFN_CONV_W + FFN_CONV - 1)
        for d in range(1, FFN_CONV):
            rows = slice(FFN_HALO - d * SUBLANES, FFN_HALO - d * SUBLANES + H)
            uc = uc + u_ref[sb, rows, :] * vec(V_FFN_CONV_W + FFN_CONV - 1 - d)
        act = (_gelu_tanh(uc[:, :D_FF]) * uc[:, D_FF:]).astype(BF16)
        _to_token_major(_dot(act, wdown_ref[...]), slab_ref.at[sb], y_ref, sb * H)
        rows = slice(sb * H, (sb + 1) * H)
        out_ref[rows, :] = _rms_norm(x_ref[rows, :] + y_ref[rows, :], vec(V_FINAL_NORM, D_MODEL))


def _const_spec(shape):
    nd = len(shape)
    return pl.BlockSpec(shape, lambda i, _nd=nd: (0,) * _nd, pipeline_mode=pl.Buffered(1))


def _pad_cols(w, n):
    return jnp.pad(w, ((0, 0), (0, n - w.shape[1])))


def _block_rows(w):
    rows = w.reshape(LRU_WIDTH, LRU_BW).astype(F32)
    return jnp.pad(rows, ((0, LRU_PAD - LRU_WIDTH), (0, MXU_DIM - LRU_BW)))


def _pack_rows(vectors, n):
    return jnp.concatenate([_pad_cols(v.reshape(-1, v.shape[-1]).astype(F32), n) for v in vectors], axis=0)


@jax.jit
def _forward(x, attn_norm, w_in, gla_wf2, gla_bf2, gla_onorm, gla_wo, lru_conv_w, lru_conv_b, lru_wa, lru_ba,
             lru_wx, lru_bx, lru_a_param, lru_wo, w_out, ffn_norm, ffn_wup, ffn_conv_w, ffn_conv_b, ffn_wdown,
             final_norm):
    B, S, D = x.shape
    assert B == 1 and D == D_MODEL and S % SEQ_BLOCK == 0 and attn_norm.shape[0] == 1
    T = SEQ_BLOCK
    x2 = x.reshape(S, D)

    w_in_t = w_in[0].T
    wf2 = jnp.pad(gla_wf2[0], ((0, RANK_PAD - GLA_GATE_RANK), (0, 0)))
    lwo = jnp.pad(lru_wo[0], ((0, LRU_PAD - LRU_WIDTH), (0, 0)))

    mixer_vec = _pack_rows([attn_norm[0], gla_bf2[0], gla_onorm[0], lru_conv_b[0], lru_ba[0], lru_bx[0],
                            lru_a_param[0], lru_conv_w[0]], LRU_PAD)
    mixer_in = [
        x2, mixer_vec, w_in_t, wf2.astype(BF16), gla_wo[0].astype(BF16),
        _block_rows(lru_wa[0]), _block_rows(lru_wx[0]), lwo.astype(BF16), w_out[0].astype(BF16),
    ]
    blk = pl.BlockSpec((T, D), lambda i: (i, 0))
    hbm = pl.BlockSpec(memory_space=pl.ANY)
    params = pltpu.CompilerParams(dimension_semantics=("arbitrary",), vmem_limit_bytes=VMEM_LIMIT_BYTES)
    x1 = pl.pallas_call(
        _mixer_kernel,
        grid=(S // T,),
        in_specs=[blk, _const_spec(mixer_vec.shape), hbm] + [_const_spec(a.shape) for a in mixer_in[3:]],
        out_specs=blk,
        out_shape=jax.ShapeDtypeStruct((S, D), F32),
        scratch_shapes=[
            pltpu.VMEM((GLA_HEADS, GLA_DV, GLA_DK), F32),
            pltpu.VMEM((LRU_HALO + T, LRU_PAD), F32),
            pltpu.VMEM((LRU_HALO, LRU_PAD), F32),
            pltpu.VMEM((1, LRU_PAD), F32),
            pltpu.VMEM((T, GLA_VAL), F32),
            pltpu.VMEM((T, LRU_PAD), F32),
            pltpu.VMEM((T, LRU_PAD), F32),
            pltpu.VMEM((D // LANES, T, LANES), F32),
            pltpu.VMEM((T, D), F32),
            pltpu.VMEM((BAND_ROWS, 2 * LANES), BF16),
            pltpu.VMEM((D, O_F - O_Q), BF16),
            pltpu.VMEM((D, RANK_PAD), BF16),
            pltpu.VMEM((D, LRU_PAD), BF16),
            pltpu.VMEM((D, LRU_PAD), BF16),
            pltpu.VMEM((D, O_END - O_GA), BF16),
            pltpu.VMEM((2, W_IN_CHUNK_ROWS, D), F32),
            pltpu.SemaphoreType.DMA((2,)),
        ],
        compiler_params=params,
        name="mixer",
    )(*mixer_in)

    ffn_vec = _pack_rows([ffn_norm[0], final_norm, ffn_conv_b[0], ffn_conv_w[0]], 2 * D_FF)
    ffn_in = [x1, ffn_vec, ffn_wup[0], ffn_wdown[0]]
    assert S % (FFN_SUB * T) == 0
    ffn_blk = pl.BlockSpec((FFN_SUB * T, D), lambda i: (i, 0))
    hbm = pl.BlockSpec(memory_space=pl.ANY)
    out = pl.pallas_call(
        _ffn_kernel,
        grid=(S // (FFN_SUB * T),),
        in_specs=[ffn_blk, _const_spec(ffn_vec.shape), hbm, hbm],
        out_specs=ffn_blk,
        out_shape=jax.ShapeDtypeStruct((S, D), F32),
        scratch_shapes=[
            pltpu.VMEM((FFN_SUB, FFN_HALO + T, 2 * D_FF), F32),
            pltpu.VMEM((FFN_HALO, 2 * D_FF), F32),
            pltpu.VMEM((FFN_SUB, D // LANES, T, LANES), F32),
            pltpu.VMEM((FFN_SUB * T, D), F32),
            pltpu.VMEM((D, 2 * D_FF), BF16),
            pltpu.VMEM((D_FF, D), BF16),
            pltpu.VMEM((2, D // WEIGHT_CHUNKS, 2 * D_FF), F32),
            pltpu.VMEM((2, D_FF // WEIGHT_CHUNKS, D), F32),
            pltpu.SemaphoreType.DMA((2,)),
        ],
        compiler_params=params,
        name="ffn",
    )(*ffn_in)
    return out.reshape(B, S, D)


def kernel(x, attn_norm, w_in, gla_wf2, gla_bf2, gla_onorm, gla_wo, lru_conv_w, lru_conv_b, lru_wa, lru_ba, lru_wx, lru_bx, lru_a_param, lru_wo, w_out, ffn_norm, ffn_wup, ffn_conv_w, ffn_conv_b, ffn_wdown, final_norm):
    return _forward(x, attn_norm, w_in, gla_wf2, gla_bf2, gla_onorm, gla_wo, lru_conv_w, lru_conv_b, lru_wa, lru_ba,
                    lru_wx, lru_bx, lru_a_param, lru_wo, w_out, ffn_norm, ffn_wup, ffn_conv_w, ffn_conv_b,
                    ffn_wdown, final_norm)
```
